```python
import math
import jax, jax.numpy as jnp
from jax import lax
import numpy as np

D_MODEL = 1024
BATCH = 8
SEQ = 2048
DEPTH = 2

HEAD_DIM = 64
SWA_Q_HEADS = 8
SWA_KV_HEADS = 2
WINDOW = 128
BLOCK = 128
DIFF_HEADS = 4
DIFF_QK_DIM = 64
DIFF_V_DIM = 2 * DIFF_QK_DIM
D_FF = ((8 * D_MODEL // 3 + 127) // 128) * 128
CONV_WIDTH = 3
ROPE_THETA = 10000.0
EPS = 1e-6
NEG = -1e30

SWA_Q = SWA_Q_HEADS * HEAD_DIM
SWA_KV = SWA_KV_HEADS * HEAD_DIM
DIFF_Q = DIFF_HEADS * 2 * DIFF_QK_DIM
DIFF_V = DIFF_HEADS * DIFF_V_DIM
IN_COLS = SWA_Q + 2 * SWA_KV + 2 * DIFF_Q + DIFF_V
MIX_WIDTH = SWA_Q + DIFF_V

kernel_name = "hybrid_swa_sink_diffattn_convglu"


def rmsnorm(x, g):
    xf = x.astype(jnp.float32)
    y = xf * lax.rsqrt(jnp.mean(xf * xf, axis=-1, keepdims=True) + EPS)
    return (y * g.astype(jnp.float32)).astype(x.dtype)


def rope_tables(seq, dim):
    inv = 1.0 / (ROPE_THETA ** (jnp.arange(0, dim, 2, dtype=jnp.float32) / dim))
    ang = jnp.arange(seq, dtype=jnp.float32)[:, None] * inv[None, :]
    return jnp.cos(ang), jnp.sin(ang)


def apply_rope(x, cos, sin):
    x1, x2 = jnp.split(x.astype(jnp.float32), 2, axis=-1)
    c = cos[None, :, None, :]
    s = sin[None, :, None, :]
    return jnp.concatenate([x1 * c - x2 * s, x2 * c + x1 * s], axis=-1).astype(x.dtype)


def windowed_gqa_sink(q, k, v, sink):
    B, S, Hq, D = q.shape
    Hkv = k.shape[2]
    G = Hq // Hkv
    nb = S // BLOCK
    qb = q.reshape(B, nb, BLOCK, Hkv, G, D)

    def band(t):
        tp = jnp.pad(t, ((0, 0), (BLOCK, BLOCK), (0, 0), (0, 0)))
        tb = tp.reshape(B, nb + 2, BLOCK, Hkv, D)
        return jnp.concatenate([tb[:, :-2], tb[:, 1:-1], tb[:, 2:]], axis=2)

    kb, vb = band(k), band(v)
    scores = jnp.einsum('bnqhgd,bnkhd->bnhgqk', qb, kb,
                        preferred_element_type=jnp.float32) * (D ** -0.5)
    qpos = jnp.arange(nb)[:, None] * BLOCK + jnp.arange(BLOCK)[None, :]
    kpos = (jnp.arange(nb)[:, None] - 1) * BLOCK + jnp.arange(3 * BLOCK)[None, :]
    rel = kpos[:, None, :] - qpos[:, :, None]
    valid = (jnp.abs(rel) <= WINDOW) & (kpos[:, None, :] >= 0) & (kpos[:, None, :] < S)
    scores = jnp.where(valid[None, :, None, None], scores, NEG)
    sink_l = sink.astype(jnp.float32).reshape(Hkv, G)[None, None, :, :, None, None]
    m = jnp.maximum(jnp.max(scores, axis=-1, keepdims=True), sink_l)
    p = jnp.exp(scores - m)
    p = p / (jnp.sum(p, axis=-1, keepdims=True) + jnp.exp(sink_l - m))
    out = jnp.einsum('bnhgqk,bnkhd->bnqhgd', p.astype(v.dtype), vb)
    return out.reshape(B, S, Hq, D)


def diff_attention(q, k, v, lam):
    B, S, H, _, Dk = q.shape
    nb = S // BLOCK
    qb = jnp.moveaxis(q.reshape(B, nb, BLOCK, H, 2, Dk), 1, 0)
    scale = Dk ** -0.5

    def one_block(qi):
        s = jnp.einsum('bqhcd,bkhcd->bhcqk', qi, k,
                       preferred_element_type=jnp.float32) * scale
        p = jax.nn.softmax(s, axis=-1)
        w = p[:, :, 0] - lam * p[:, :, 1]
        return jnp.einsum('bhqk,bkhd->bqhd', w.astype(v.dtype), v)

    out = lax.map(one_block, qb)
    return jnp.moveaxis(out, 0, 1).reshape(B, S, H, v.shape[-1])


def centred_dwconv(x, w, b):
    S = x.shape[1]
    half = CONV_WIDTH // 2
    xp = jnp.pad(x, ((0, 0), (half, half), (0, 0)))
    out = b
    for j in range(CONV_WIDTH):
        out = out + xp[:, j:j + S, :] * w[j]
    return out


def setup_inputs(seed: int = 0) -> dict:
    key = jax.random.key(seed)
    ks = jax.random.split(key, 20)
    f32 = jnp.float32
    nrm = lambda k, shp, sc: jax.random.normal(k, shp, f32) * sc
    L = DEPTH
    return {
        "x": nrm(ks[0], (BATCH, SEQ, D_MODEL), 1.0),
        "g_attn": 1.0 + nrm(ks[1], (L, D_MODEL), 0.02),
        "w_in": nrm(ks[2], (L, D_MODEL, IN_COLS), D_MODEL ** -0.5),
        "qn_a": 1.0 + nrm(ks[3], (L, HEAD_DIM), 0.02),
        "kn_a": 1.0 + nrm(ks[4], (L, HEAD_DIM), 0.02),
        "sink": nrm(ks[5], (L, SWA_Q_HEADS), 0.5),
        "qn_b": 1.0 + nrm(ks[6], (L, DIFF_QK_DIM), 0.02),
        "kn_b": 1.0 + nrm(ks[7], (L, DIFF_QK_DIM), 0.02),
        "lq1": nrm(ks[8], (L, DIFF_QK_DIM), 0.1),
        "lk1": nrm(ks[9], (L, DIFF_QK_DIM), 0.1),
        "lq2": nrm(ks[10], (L, DIFF_QK_DIM), 0.1),
        "lk2": nrm(ks[11], (L, DIFF_QK_DIM), 0.1),
        "subln": 1.0 + nrm(ks[12], (L, DIFF_V_DIM), 0.02),
        "w_out": nrm(ks[13], (L, MIX_WIDTH, D_MODEL), MIX_WIDTH ** -0.5),
        "g_ffn": 1.0 + nrm(ks[14], (L, D_MODEL), 0.02),
        "w_up": nrm(ks[15], (L, D_MODEL, 2 * D_FF), D_MODEL ** -0.5),
        "conv_w": nrm(ks[16], (L, CONV_WIDTH, D_FF), CONV_WIDTH ** -0.5),
        "conv_b": nrm(ks[17], (L, D_FF), 0.02),
        "w_down": nrm(ks[18], (L, D_FF, D_MODEL), D_FF ** -0.5),
    }


def reference(x, g_attn, w_in, qn_a, kn_a, sink, qn_b, kn_b, lq1, lk1, lq2, lk2,
              subln, w_out, g_ffn, w_up, conv_w, conv_b, w_down):
    B, S, _ = x.shape
    cos_a, sin_a = rope_tables(S, HEAD_DIM)
    cos_b, sin_b = rope_tables(S, DIFF_QK_DIM)
    offs = np.cumsum([SWA_Q, SWA_KV, SWA_KV, DIFF_Q, DIFF_Q]).tolist()
    for l in range(DEPTH):
        lambda_init = 0.8 - 0.6 * math.exp(-0.3 * l)
        h = rmsnorm(x, g_attn[l])
        proj = jnp.einsum('bsd,dc->bsc', h, w_in[l])
        qa, ka, va, qb, kb, vb = jnp.split(proj, offs, axis=-1)
        qa = apply_rope(rmsnorm(qa.reshape(B, S, SWA_Q_HEADS, HEAD_DIM), qn_a[l]), cos_a, sin_a)
        ka = apply_rope(rmsnorm(ka.reshape(B, S, SWA_KV_HEADS, HEAD_DIM), kn_a[l]), cos_a, sin_a)
        va = va.reshape(B, S, SWA_KV_HEADS, HEAD_DIM)
        ya = windowed_gqa_sink(qa, ka, va, sink[l])
        qb = apply_rope(rmsnorm(qb.reshape(B, S, 2 * DIFF_HEADS, DIFF_QK_DIM), qn_b[l]), cos_b, sin_b)
        kb = apply_rope(rmsnorm(kb.reshape(B, S, 2 * DIFF_HEADS, DIFF_QK_DIM), kn_b[l]), cos_b, sin_b)
        qb = qb.reshape(B, S, DIFF_HEADS, 2, DIFF_QK_DIM)
        kb = kb.reshape(B, S, DIFF_HEADS, 2, DIFF_QK_DIM)
        vb = vb.reshape(B, S, DIFF_HEADS, DIFF_V_DIM)
        lam = (jnp.exp(jnp.sum(lq1[l].astype(jnp.float32) * lk1[l].astype(jnp.float32)))
               - jnp.exp(jnp.sum(lq2[l].astype(jnp.float32) * lk2[l].astype(jnp.float32)))
               + lambda_init)
        yb = diff_attention(qb, kb, vb, lam)
        yb = rmsnorm(yb, subln[l]) * (1.0 - lambda_init)
        y = jnp.concatenate([ya.reshape(B, S, SWA_Q), yb.reshape(B, S, DIFF_V)], axis=-1)
        x = x + jnp.einsum('bsm,md->bsd', y, w_out[l])
        h = rmsnorm(x, g_ffn[l])
        gate, val = jnp.split(jnp.einsum('bsd,df->bsf', h, w_up[l]), 2, axis=-1)
        gate = centred_dwconv(gate, conv_w[l], conv_b[l])
        x = x + jnp.einsum('bsf,fd->bsd', jax.nn.silu(gate) * val, w_down[l])
    return x
```

```python
import functools
import math

import jax
import jax.numpy as jnp
from jax import lax
from jax.experimental import pallas as pl
from jax.experimental.pallas import tpu as pltpu

D_MODEL = 1024
HEAD_DIM = 64
SWA_Q_HEADS = 8
SWA_KV_HEADS = 2
WINDOW = 128
BLOCK = 128
DIFF_HEADS = 4
DIFF_V_DIM = 128
D_FF = 2816
ROPE_THETA = 10000.0
EPS = 1e-6
NEG = -1e30

SWA_Q = SWA_Q_HEADS * HEAD_DIM
SWA_KV = SWA_KV_HEADS * HEAD_DIM
DIFF_Q = DIFF_HEADS * 2 * HEAD_DIM
DIFF_V = DIFF_HEADS * DIFF_V_DIM

LANES = 128
MXU_COLS = 256
CHUNK = MXU_COLS
PROJ_COLS = 2560
N_PROJ_CHUNKS = PROJ_COLS // CHUNK
NORM_CHUNKS = (0, 1, 2, 4, 5, 6, 7)
FF_CHUNKS = D_FF // CHUNK
HALO = 16

BF16 = jnp.bfloat16
F32 = jnp.float32


def _rms(x):
    return lax.rsqrt(jnp.mean(x * x, axis=-1, keepdims=True) + EPS)


def _inproj_kernel(x_ref, g_ref, w_ref, gcol_ref, cos_ref, sin_ref, bd_ref, o_ref):
    x = x_ref[...]
    h = (x * _rms(x) * g_ref[...]).astype(BF16)
    cos = cos_ref[...]
    sin = sin_ref[...]
    lane = lax.broadcasted_iota(jnp.int32, cos.shape, 1)
    first_half = (lane % HEAD_DIM) < (HEAD_DIM // 2)
    for c in range(N_PROJ_CHUNKS):
        cols = slice(c * CHUNK, (c + 1) * CHUNK)
        p = jnp.dot(h, w_ref[:, cols], preferred_element_type=F32)
        if c in NORM_CHUNKS:
            ss = jnp.dot((p * p).astype(BF16), bd_ref[...], preferred_element_type=F32)
            y = p * lax.rsqrt(ss * (1.0 / HEAD_DIM) + EPS) * gcol_ref[:, cols]
            partner = jnp.where(first_half,
                                pltpu.roll(y, CHUNK - HEAD_DIM // 2, 1),
                                pltpu.roll(y, HEAD_DIM // 2, 1))
            p = y * cos + partner * sin
        o_ref[:, cols] = p.astype(BF16)


def _inproj(xf, g, w, gcol, cos_t, sin_t, bd, *, seq, tm):
    t = xf.shape[0]
    tiles_per_seq = seq // tm
    return pl.pallas_call(
        _inproj_kernel,
        grid=(t // tm,),
        in_specs=[
            pl.BlockSpec((tm, D_MODEL), lambda i: (i, 0)),
            pl.BlockSpec((1, D_MODEL), lambda i: (0, 0)),
            pl.BlockSpec((D_MODEL, PROJ_COLS), lambda i: (0, 0)),
            pl.BlockSpec((1, PROJ_COLS), lambda i: (0, 0)),
            pl.BlockSpec((tm, CHUNK), lambda i: (i % tiles_per_seq, 0)),
            pl.BlockSpec((tm, CHUNK), lambda i: (i % tiles_per_seq, 0)),
            pl.BlockSpec((CHUNK, CHUNK), lambda i: (0, 0)),
        ],
        out_specs=pl.BlockSpec((tm, PROJ_COLS), lambda i: (i, 0)),
        out_shape=jax.ShapeDtypeStruct((t, PROJ_COLS), BF16),
        compiler_params=pltpu.CompilerParams(
            dimension_semantics=("parallel",), vmem_limit_bytes=48 * 1024 * 1024),
        name="inproj",
    )(xf, g, w, gcol, cos_t, sin_t, bd)


def _swa_kernel(sink_ref, q_ref, kp_ref, kc_ref, kn_ref, vp_ref, vc_ref, vn_ref, o_ref, *, nblocks):
    n = pl.program_id(1)
    k = jnp.concatenate([kp_ref[...], kc_ref[...], kn_ref[...]], axis=0)
    v = jnp.concatenate([vp_ref[...], vc_ref[...], vn_ref[...]], axis=0)
    row = lax.broadcasted_iota(jnp.int32, (BLOCK, 3 * BLOCK), 0)
    col = lax.broadcasted_iota(jnp.int32, (BLOCK, 3 * BLOCK), 1)
    rel = col - BLOCK - row
    valid = ((jnp.abs(rel) <= WINDOW)
             & ((col >= BLOCK) | (n > 0))
             & ((col < 2 * BLOCK) | (n < nblocks - 1)))
    lo = lax.broadcasted_iota(jnp.int32, (BLOCK, LANES), 1) < HEAD_DIM
    heads_per_kv = SWA_Q_HEADS // SWA_KV_HEADS
    for j in range(SWA_Q // LANES):
        qj = q_ref[:, j * LANES:(j + 1) * LANES]
        kv = (2 * j) // heads_per_kv
        kk = k[:, kv * LANES:(kv + 1) * LANES]
        vv = v[:, kv * LANES:(kv + 1) * LANES]
        outs = []
        for half in range(2):
            head = 2 * j + half
            qm = jnp.where(lo if half == 0 else jnp.logical_not(lo), qj, jnp.zeros_like(qj))
            s = lax.dot_general(qm, kk, (((1,), (1,)), ((), ())), preferred_element_type=F32)
            s = jnp.where(valid, s, NEG)
            sk = sink_ref[head]
            m = jnp.maximum(jnp.max(s, axis=-1, keepdims=True), sk)
            p = jnp.exp(s - m)
            denom = jnp.sum(p, axis=-1, keepdims=True) + jnp.exp(sk - m)
            o = jnp.dot(p.astype(BF16), vv, preferred_element_type=F32)
            outs.append(o / denom)
        o_ref[:, j * LANES:(j + 1) * LANES] = jnp.where(lo, outs[0], outs[1]).astype(BF16)


def _swa(proj3, sink):
    b, s, _ = proj3.shape
    nb = s // BLOCK
    kcol = SWA_Q // CHUNK
    vcol = kcol + 1
    prev = lambda n: jnp.maximum(n - 1, 0)
    nxt = lambda n: jnp.minimum(n + 1, nb - 1)
    band = lambda col, f: pl.BlockSpec((None, BLOCK, CHUNK), lambda bi, n: (bi, f(n), col))
    same = lambda n: n
    return pl.pallas_call(
        functools.partial(_swa_kernel, nblocks=nb),
        grid=(b, nb),
        in_specs=[
            pl.BlockSpec(memory_space=pltpu.SMEM),
            pl.BlockSpec((None, BLOCK, SWA_Q), lambda bi, n: (bi, n, 0)),
            band(kcol, prev), band(kcol, same), band(kcol, nxt),
            band(vcol, prev), band(vcol, same), band(vcol, nxt),
        ],
        out_specs=pl.BlockSpec((None, BLOCK, SWA_Q), lambda bi, n: (bi, n, 0)),
        out_shape=jax.ShapeDtypeStruct((b, s, SWA_Q), BF16),
        compiler_params=pltpu.CompilerParams(dimension_semantics=("parallel", "parallel")),
        name="swa",
    )(sink, proj3, proj3, proj3, proj3, proj3, proj3, proj3)


def _diff_kernel(lq1_ref, lk1_ref, lq2_ref, lk2_ref, subln_ref, q_ref, k_ref, v_ref, o_ref, *,
                 lambda_init):
    lam = (jnp.exp(jnp.sum(lq1_ref[...] * lk1_ref[...], axis=-1, keepdims=True))
           - jnp.exp(jnp.sum(lq2_ref[...] * lk2_ref[...], axis=-1, keepdims=True))
           + lambda_init)
    q = q_ref[...]
    k = k_ref[...]
    v = v_ref[...]
    lo = lax.broadcasted_iota(jnp.int32, q.shape, 1) < HEAD_DIM

    def attend(qm):
        s = lax.dot_general(qm, k, (((1,), (1,)), ((), ())), preferred_element_type=F32)
        m = jnp.max(s, axis=-1, keepdims=True)
        p = jnp.exp(s - m)
        denom = jnp.sum(p, axis=-1, keepdims=True)
        return jnp.dot(p.astype(BF16), v, preferred_element_type=F32) / denom

    zero = jnp.zeros_like(q)
    o = attend(jnp.where(lo, q, zero)) - lam * attend(jnp.where(lo, zero, q))
    y = o * _rms(o) * subln_ref[...] * (1.0 - lambda_init)
    o_ref[...] = y.astype(BF16)


def _diff(proj3, lq1, lk1, lq2, lk2, subln, *, lambda_init, tq):
    b, s, _ = proj3.shape
    qcol = (2 * SWA_Q) // LANES
    kcol = qcol + DIFF_Q // LANES
    vcol = kcol + DIFF_Q // LANES
    small = lambda w: pl.BlockSpec((1, w), lambda bi, h, i: (0, 0))
    return pl.pallas_call(
        functools.partial(_diff_kernel, lambda_init=lambda_init),
        grid=(b, DIFF_HEADS, s // tq),
        in_specs=[
            small(HEAD_DIM), small(HEAD_DIM), small(HEAD_DIM), small(HEAD_DIM), small(DIFF_V_DIM),
            pl.BlockSpec((None, tq, LANES), lambda bi, h, i: (bi, i, qcol + h)),
            pl.BlockSpec((None, s, LANES), lambda bi, h, i: (bi, 0, kcol + h)),
            pl.BlockSpec((None, s, LANES), lambda bi, h, i: (bi, 0, vcol + h)),
        ],
        out_specs=pl.BlockSpec((None, tq, DIFF_V_DIM), lambda bi, h, i: (bi, i, h)),
        out_shape=jax.ShapeDtypeStruct((b, s, DIFF_V), BF16),
        compiler_params=pltpu.CompilerParams(
            dimension_semantics=("parallel", "parallel", "parallel"),
            vmem_limit_bytes=48 * 1024 * 1024),
        name="diffattn",
    )(lq1, lk1, lq2, lk2, subln, proj3, proj3, proj3)


def _outproj_kernel(x_ref, ya_ref, yb_ref, wa_ref, wb_ref, o_ref):
    acc = x_ref[...] + jnp.dot(ya_ref[...], wa_ref[...], preferred_element_type=F32)
    o_ref[...] = acc + jnp.dot(yb_ref[...], wb_ref[...], preferred_element_type=F32)


def _outproj(xf, ya, yb, wa, wb, *, tm):
    t = xf.shape[0]
    return pl.pallas_call(
        _outproj_kernel,
        grid=(t // tm,),
        in_specs=[
            pl.BlockSpec((tm, D_MODEL), lambda i: (i, 0)),
            pl.BlockSpec((tm, SWA_Q), lambda i: (i, 0)),
            pl.BlockSpec((tm, DIFF_V), lambda i: (i, 0)),
            pl.BlockSpec((SWA_Q, D_MODEL), lambda i: (0, 0)),
            pl.BlockSpec((DIFF_V, D_MODEL), lambda i: (0, 0)),
        ],
        out_specs=pl.BlockSpec((tm, D_MODEL), lambda i: (i, 0)),
        out_shape=jax.ShapeDtypeStruct((t, D_MODEL), F32),
        compiler_params=pltpu.CompilerParams(dimension_semantics=("parallel",)),
        name="outproj",
    )(xf, ya, yb, wa, wb)


def _ffn_kernel(xp_ref, xc_ref, xn_ref, g_ref, wg_ref, wv_ref, cw_ref, cb_ref, wd_ref, o_ref,
                hext_ref, *, ts, tiles_per_seq):
    j = pl.program_id(0) % tiles_per_seq
    g = g_ref[...]

    def norm(xx):
        return (xx * _rms(xx) * g).astype(BF16)

    hp = norm(xp_ref[...])
    hn = norm(xn_ref[...])
    hext_ref[0:HALO, :] = jnp.where(j > 0, hp, jnp.zeros_like(hp))
    xc = xc_ref[...]
    hext_ref[HALO:HALO + ts, :] = norm(xc)
    hext_ref[HALO + ts:, :] = jnp.where(j < tiles_per_seq - 1, hn, jnp.zeros_like(hn))
    o_ref[...] = xc

    def body(c, carry):
        gate = jnp.dot(hext_ref[...], wg_ref[c], preferred_element_type=F32)
        val = jnp.dot(hext_ref[HALO:HALO + ts, :], wv_ref[c], preferred_element_type=F32)
        cw = cw_ref[c]
        rows = gate.shape[0]
        g_prev = pltpu.roll(gate, 1, 0)[HALO:HALO + ts]
        g_next = pltpu.roll(gate, rows - 1, 0)[HALO:HALO + ts]
        g_cur = gate[HALO:HALO + ts]
        conv = cb_ref[c] + cw[0:1] * g_prev + cw[1:2] * g_cur + cw[2:3] * g_next
        act = conv * (1.0 / (1.0 + jnp.exp(-conv))) * val
        o_ref[...] += jnp.dot(act.astype(BF16), wd_ref[c], preferred_element_type=F32)
        return carry

    lax.fori_loop(0, FF_CHUNKS, body, 0)


def _ffn(xf, g, wg, wv, cw, cb, wd, *, seq, ts):
    t = xf.shape[0]
    tiles_per_seq = seq // ts
    halo_blocks = ts // HALO
    last_halo = t // HALO - 1
    const3 = lambda shape: pl.BlockSpec(shape, lambda i: (0, 0, 0), pipeline_mode=pl.Buffered(1))
    return pl.pallas_call(
        functools.partial(_ffn_kernel, ts=ts, tiles_per_seq=tiles_per_seq),
        grid=(t // ts,),
        in_specs=[
            pl.BlockSpec((HALO, D_MODEL), lambda i: (jnp.maximum(i * halo_blocks - 1, 0), 0)),
            pl.BlockSpec((ts, D_MODEL), lambda i: (i, 0)),
            pl.BlockSpec((HALO, D_MODEL), lambda i: (jnp.minimum((i + 1) * halo_blocks, last_halo), 0)),
            pl.BlockSpec((1, D_MODEL), lambda i: (0, 0)),
            const3((FF_CHUNKS, D_MODEL, CHUNK)),
            const3((FF_CHUNKS, D_MODEL, CHUNK)),
            const3((FF_CHUNKS, 3, CHUNK)),
            const3((FF_CHUNKS, 1, CHUNK)),
            const3((FF_CHUNKS, CHUNK, D_MODEL)),
        ],
        out_specs=pl.BlockSpec((ts, D_MODEL), lambda i: (i, 0)),
        out_shape=jax.ShapeDtypeStruct((t, D_MODEL), F32),
        scratch_shapes=[pltpu.VMEM((ts + 2 * HALO, D_MODEL), BF16)],
        compiler_params=pltpu.CompilerParams(
            dimension_semantics=("parallel",), vmem_limit_bytes=48 * 1024 * 1024),
        name="convglu",
    )(xf, xf, xf, g, wg, wv, cw, cb, wd)


def _rope_tables(seq):
    inv = 1.0 / (ROPE_THETA ** (jnp.arange(0, HEAD_DIM, 2, dtype=F32) / HEAD_DIM))
    ang = jnp.arange(seq, dtype=F32)[:, None] * inv[None, :]
    cos, sin = jnp.cos(ang), jnp.sin(ang)
    reps = CHUNK // HEAD_DIM
    cos_t = jnp.tile(jnp.concatenate([cos, cos], axis=-1), (1, reps))
    sin_t = jnp.tile(jnp.concatenate([-sin, sin], axis=-1), (1, reps))
    return cos_t, sin_t


def _head_block_diag():
    idx = jnp.arange(CHUNK) // HEAD_DIM
    return (idx[:, None] == idx[None, :]).astype(BF16)


def kernel(x, g_attn, w_in, qn_a, kn_a, sink, qn_b, kn_b, lq1, lk1, lq2, lk2, subln, w_out, g_ffn,
           w_up, conv_w, conv_b, w_down):
    b, s, d = x.shape
    depth = w_in.shape[0]
    cos_t, sin_t = _rope_tables(s)
    bd = _head_block_diag()
    scale = HEAD_DIM ** -0.5
    xf = x.reshape(b * s, d)
    o_q, o_k, o_v, o_qb, o_kb = SWA_Q, SWA_Q + SWA_KV, SWA_Q + 2 * SWA_KV, SWA_Q + 2 * SWA_KV + DIFF_Q, \
        SWA_Q + 2 * SWA_KV + 2 * DIFF_Q

    def dup_heads(w):
        return jnp.concatenate([w[:, :HEAD_DIM], w[:, :HEAD_DIM], w[:, HEAD_DIM:], w[:, HEAD_DIM:]], axis=1)

    for l in range(depth):
        lambda_init = 0.8 - 0.6 * math.exp(-0.3 * l)
        wl = w_in[l]
        w_proj = jnp.concatenate(
            [wl[:, :o_q], dup_heads(wl[:, o_q:o_k]), dup_heads(wl[:, o_k:o_v]), wl[:, o_v:]],
            axis=1).astype(BF16)
        gcol = jnp.concatenate([
            jnp.tile(qn_a[l], SWA_Q_HEADS) * scale, jnp.tile(kn_a[l], 2 * SWA_KV_HEADS),
            jnp.ones((2 * SWA_KV,), F32),
            jnp.tile(qn_b[l], 2 * DIFF_HEADS) * scale, jnp.tile(kn_b[l], 2 * DIFF_HEADS),
            jnp.ones((DIFF_V,), F32)])[None, :]
        proj = _inproj(xf, g_attn[l][None, :], w_proj, gcol, cos_t, sin_t, bd, seq=s, tm=512)
        proj3 = proj.reshape(b, s, PROJ_COLS)
        ya = _swa(proj3, sink[l])
        yb = _diff(proj3, lq1[l][None, :], lk1[l][None, :], lq2[l][None, :], lk2[l][None, :],
                   subln[l][None, :], lambda_init=lambda_init, tq=256)
        wo = w_out[l].astype(BF16)
        xf = _outproj(xf, ya.reshape(b * s, SWA_Q), yb.reshape(b * s, DIFF_V),
                      wo[:SWA_Q], wo[SWA_Q:], tm=512)
        wu = w_up[l].astype(BF16)
        wg = wu[:, :D_FF].reshape(d, FF_CHUNKS, CHUNK).transpose(1, 0, 2)
        wv = wu[:, D_FF:].reshape(d, FF_CHUNKS, CHUNK).transpose(1, 0, 2)
        cw = conv_w[l].reshape(3, FF_CHUNKS, CHUNK).transpose(1, 0, 2)
        cb = conv_b[l].reshape(FF_CHUNKS, 1, CHUNK)
        wd = w_down[l].astype(BF16).reshape(FF_CHUNKS, CHUNK, d)
        xf = _ffn(xf, g_ffn[l][None, :], wg, wv, cw, cb, wd, seq=s, ts=512)
    return xf.reshape(b, s, d)
```

```python
import functools
import math

import jax
import jax.numpy as jnp
from jax import lax
from jax.experimental import pallas as pl
from jax.experimental.pallas import tpu as pltpu

D_MODEL = 1024
HEAD_DIM = 64
SWA_Q_HEADS = 8
SWA_KV_HEADS = 2
WINDOW = 128
BLOCK = 128
DIFF_HEADS = 4
DIFF_V_DIM = 128
D_FF = 2816
ROPE_THETA = 10000.0
EPS = 1e-6
NEG = -1e30
LOG2E = 1.4426950408889634

SWA_Q = SWA_Q_HEADS * HEAD_DIM
SWA_KV = SWA_KV_HEADS * HEAD_DIM
DIFF_Q = DIFF_HEADS * 2 * HEAD_DIM
DIFF_V = DIFF_HEADS * DIFF_V_DIM

LANES = 128
MXU_COLS = 256
CHUNK = MXU_COLS
PROJ_COLS = 2560
N_PROJ_CHUNKS = PROJ_COLS // CHUNK
NORM_CHUNKS = (0, 1, 2, 4, 5, 6, 7)
FF_CHUNKS = D_FF // CHUNK
HALO = 16

BF16 = jnp.bfloat16
F32 = jnp.float32


def _rms(x):
    return lax.rsqrt(jnp.mean(x * x, axis=-1, keepdims=True) + EPS)


def _inproj_kernel(x_ref, g_ref, w_ref, gcol_ref, cos_ref, sin_ref, bd_ref, o_ref):
    x = x_ref[...]
    h = (x * _rms(x) * g_ref[...]).astype(BF16)
    cos = cos_ref[...]
    sin = sin_ref[...]
    lane = lax.broadcasted_iota(jnp.int32, cos.shape, 1)
    first_half = (lane % HEAD_DIM) < (HEAD_DIM // 2)

    def project(c):
        return jnp.dot(h, w_ref[:, c * CHUNK:(c + 1) * CHUNK], preferred_element_type=F32)

    p_next = project(0)
    for c in range(N_PROJ_CHUNKS):
        cols = slice(c * CHUNK, (c + 1) * CHUNK)
        p = p_next
        if c + 1 < N_PROJ_CHUNKS:
            p_next = project(c + 1)
        if c in NORM_CHUNKS:
            ss = jnp.dot((p * p).astype(BF16), bd_ref[...], preferred_element_type=F32)
            y = p * lax.rsqrt(ss * (1.0 / HEAD_DIM) + EPS) * gcol_ref[:, cols]
            partner = jnp.where(first_half,
                                pltpu.roll(y, CHUNK - HEAD_DIM // 2, 1),
                                pltpu.roll(y, HEAD_DIM // 2, 1))
            p = y * cos + partner * sin
        o_ref[:, cols] = p.astype(BF16)


def _inproj(xf, g, w, gcol, cos_t, sin_t, bd, *, seq, tm):
    t = xf.shape[0]
    tiles_per_seq = seq // tm
    return pl.pallas_call(
        _inproj_kernel,
        grid=(t // tm,),
        in_specs=[
            pl.BlockSpec((tm, D_MODEL), lambda i: (i, 0)),
            pl.BlockSpec((1, D_MODEL), lambda i: (0, 0)),
            pl.BlockSpec((D_MODEL, PROJ_COLS), lambda i: (0, 0)),
            pl.BlockSpec((1, PROJ_COLS), lambda i: (0, 0)),
            pl.BlockSpec((tm, CHUNK), lambda i: (i % tiles_per_seq, 0)),
            pl.BlockSpec((tm, CHUNK), lambda i: (i % tiles_per_seq, 0)),
            pl.BlockSpec((CHUNK, CHUNK), lambda i: (0, 0)),
        ],
        out_specs=pl.BlockSpec((tm, PROJ_COLS), lambda i: (i, 0)),
        out_shape=jax.ShapeDtypeStruct((t, PROJ_COLS), BF16),
        compiler_params=pltpu.CompilerParams(
            dimension_semantics=("parallel",), vmem_limit_bytes=48 * 1024 * 1024),
        name="inproj",
    )(xf, g, w, gcol, cos_t, sin_t, bd)


def _swa_kernel(sink_ref, q_ref, kp_ref, kc_ref, kn_ref, vp_ref, vc_ref, vn_ref, o_ref, *, nblocks):
    n = pl.program_id(1)
    k = jnp.concatenate([kp_ref[...], kc_ref[...], kn_ref[...]], axis=0)
    v = jnp.concatenate([vp_ref[...], vc_ref[...], vn_ref[...]], axis=0)
    row = lax.broadcasted_iota(jnp.int32, (BLOCK, 3 * BLOCK), 0)
    col = lax.broadcasted_iota(jnp.int32, (BLOCK, 3 * BLOCK), 1)
    rel = col - BLOCK - row
    valid = ((jnp.abs(rel) <= WINDOW)
             & ((col >= BLOCK) | (n > 0))
             & ((col < 2 * BLOCK) | (n < nblocks - 1)))
    lo = lax.broadcasted_iota(jnp.int32, (BLOCK, LANES), 1) < HEAD_DIM
    hi = jnp.logical_not(lo)
    heads_per_kv = SWA_Q_HEADS // SWA_KV_HEADS
    ones = jnp.ones((3 * BLOCK, LANES), BF16)
    for kv in range(SWA_KV_HEADS):
        qg = q_ref[:, kv * heads_per_kv * HEAD_DIM:(kv + 1) * heads_per_kv * HEAD_DIM]
        zero = jnp.zeros((BLOCK, LANES), BF16)
        stack = []
        for hh in range(heads_per_kv):
            blk = qg[:, (hh // 2) * LANES:(hh // 2 + 1) * LANES]
            stack.append(jnp.where(lo if hh % 2 == 0 else hi, blk, zero))
        qs = jnp.concatenate(stack, axis=0)
        kk = k[:, kv * LANES:(kv + 1) * LANES]
        v1 = jnp.concatenate([v[:, kv * LANES:(kv + 1) * LANES], ones], axis=1)
        s = lax.dot_general(qs, kk, (((1,), (1,)), ((), ())), preferred_element_type=F32)
        s = jnp.where(valid[None], s.reshape(heads_per_kv, BLOCK, 3 * BLOCK), NEG)
        s = s.reshape(heads_per_kv * BLOCK, 3 * BLOCK)
        sk = jnp.concatenate(
            [jnp.full((BLOCK, 1), sink_ref[kv * heads_per_kv + hh] * LOG2E, F32)
             for hh in range(heads_per_kv)], axis=0)
        m = jnp.maximum(jnp.max(s, axis=-1, keepdims=True), sk)
        p = jnp.exp2(s - m).astype(BF16)
        o2 = jnp.dot(p, v1, preferred_element_type=F32)
        denom = o2[:, LANES:] + jnp.exp2(sk - m)
        o = o2[:, :LANES] / denom
        for pair in range(heads_per_kv // 2):
            blk = jnp.where(lo, o[2 * pair * BLOCK:(2 * pair + 1) * BLOCK],
                            o[(2 * pair + 1) * BLOCK:(2 * pair + 2) * BLOCK])
            col = kv * heads_per_kv * HEAD_DIM + pair * LANES
            o_ref[:, col:col + LANES] = blk.astype(BF16)


def _swa(proj3, sink):
    b, s, _ = proj3.shape
    nb = s // BLOCK
    kcol = SWA_Q // CHUNK
    vcol = kcol + 1
    prev = lambda n: jnp.maximum(n - 1, 0)
    nxt = lambda n: jnp.minimum(n + 1, nb - 1)
    band = lambda col, f: pl.BlockSpec((None, BLOCK, CHUNK), lambda bi, n: (bi, f(n), col))
    same = lambda n: n
    return pl.pallas_call(
        functools.partial(_swa_kernel, nblocks=nb),
        grid=(b, nb),
        in_specs=[
            pl.BlockSpec(memory_space=pltpu.SMEM),
            pl.BlockSpec((None, BLOCK, SWA_Q), lambda bi, n: (bi, n, 0)),
            band(kcol, prev), band(kcol, same), band(kcol, nxt),
            band(vcol, prev), band(vcol, same), band(vcol, nxt),
        ],
        out_specs=pl.BlockSpec((None, BLOCK, SWA_Q), lambda bi, n: (bi, n, 0)),
        out_shape=jax.ShapeDtypeStruct((b, s, SWA_Q), BF16),
        compiler_params=pltpu.CompilerParams(dimension_semantics=("parallel", "parallel")),
        name="swa",
    )(sink, proj3, proj3, proj3, proj3, proj3, proj3, proj3)


def _diff_kernel(lq1_ref, lk1_ref, lq2_ref, lk2_ref, subln_ref, q_ref, k_ref, v_ref, o_ref, v1_ref, *,
                 lambda_init):
    lam = (jnp.exp(jnp.sum(lq1_ref[...] * lk1_ref[...], axis=-1, keepdims=True))
           - jnp.exp(jnp.sum(lq2_ref[...] * lk2_ref[...], axis=-1, keepdims=True))
           + lambda_init)

    @pl.when(pl.program_id(2) == 0)
    def _():
        v1_ref[:, :DIFF_V_DIM] = v_ref[...]
        v1_ref[:, DIFF_V_DIM:] = jnp.ones(v_ref.shape, BF16)

    q = q_ref[...]
    k = k_ref[...]
    lo = lax.broadcasted_iota(jnp.int32, q.shape, 1) < HEAD_DIM

    def scores(qm):
        return lax.dot_general(qm, k, (((1,), (1,)), ((), ())), preferred_element_type=F32)

    def attend(s):
        m = jnp.max(s, axis=-1, keepdims=True)
        p = jnp.exp2(s - m).astype(BF16)
        o2 = jnp.dot(p, v1_ref[...], preferred_element_type=F32)
        return o2[:, :DIFF_V_DIM] / o2[:, DIFF_V_DIM:]

    zero = jnp.zeros_like(q)
    s0 = scores(jnp.where(lo, q, zero))
    s1 = scores(jnp.where(lo, zero, q))
    o = attend(s0) - lam * attend(s1)
    y = o * _rms(o) * subln_ref[...] * (1.0 - lambda_init)
    o_ref[...] = y.astype(BF16)


def _diff(proj3, lq1, lk1, lq2, lk2, subln, *, lambda_init, tq):
    b, s, _ = proj3.shape
    qcol = (2 * SWA_Q) // LANES
    kcol = qcol + DIFF_Q // LANES
    vcol = kcol + DIFF_Q // LANES
    small = lambda w: pl.BlockSpec((1, w), lambda bi, h, i: (0, 0))
    return pl.pallas_call(
        functools.partial(_diff_kernel, lambda_init=lambda_init),
        grid=(b, DIFF_HEADS, s // tq),
        in_specs=[
            small(HEAD_DIM), small(HEAD_DIM), small(HEAD_DIM), small(HEAD_DIM), small(DIFF_V_DIM),
            pl.BlockSpec((None, tq, LANES), lambda bi, h, i: (bi, i, qcol + h)),
            pl.BlockSpec((None, s, LANES), lambda bi, h, i: (bi, 0, kcol + h)),
            pl.BlockSpec((None, s, LANES), lambda bi, h, i: (bi, 0, vcol + h)),
        ],
        out_specs=pl.BlockSpec((None, tq, DIFF_V_DIM), lambda bi, h, i: (bi, i, h)),
        out_shape=jax.ShapeDtypeStruct((b, s, DIFF_V), BF16),
        scratch_shapes=[pltpu.VMEM((s, 2 * DIFF_V_DIM), BF16)],
        compiler_params=pltpu.CompilerParams(
            dimension_semantics=("parallel", "parallel", "arbitrary"),
            vmem_limit_bytes=48 * 1024 * 1024),
        name="diffattn",
    )(lq1, lk1, lq2, lk2, subln, proj3, proj3, proj3)


def _outproj_kernel(x_ref, ya_ref, yb_ref, wa_ref, wb_ref, o_ref):
    acc = x_ref[...] + jnp.dot(ya_ref[...], wa_ref[...], preferred_element_type=F32)
    o_ref[...] = acc + jnp.dot(yb_ref[...], wb_ref[...], preferred_element_type=F32)


def _outproj(xf, ya, yb, wa, wb, *, tm):
    t = xf.shape[0]
    return pl.pallas_call(
        _outproj_kernel,
        grid=(t // tm,),
        in_specs=[
            pl.BlockSpec((tm, D_MODEL), lambda i: (i, 0)),
            pl.BlockSpec((tm, SWA_Q), lambda i: (i, 0)),
            pl.BlockSpec((tm, DIFF_V), lambda i: (i, 0)),
            pl.BlockSpec((SWA_Q, D_MODEL), lambda i: (0, 0)),
            pl.BlockSpec((DIFF_V, D_MODEL), lambda i: (0, 0)),
        ],
        out_specs=pl.BlockSpec((tm, D_MODEL), lambda i: (i, 0)),
        out_shape=jax.ShapeDtypeStruct((t, D_MODEL), F32),
        compiler_params=pltpu.CompilerParams(dimension_semantics=("parallel",)),
        name="outproj",
    )(xf, ya, yb, wa, wb)


def _ffn_kernel(xp_ref, xc_ref, xn_ref, g_ref, wg_ref, wv_ref, cw_ref, cb_ref, wd_ref, o_ref,
                hext_ref, *, ts, tiles_per_seq):
    j = pl.program_id(0) % tiles_per_seq
    g = g_ref[...]

    def norm(xx):
        return (xx * _rms(xx) * g).astype(BF16)

    hp = norm(xp_ref[...])
    hn = norm(xn_ref[...])
    hext_ref[0:HALO, :] = jnp.where(j > 0, hp, jnp.zeros_like(hp))
    xc = xc_ref[...]
    hext_ref[HALO:HALO + ts, :] = norm(xc)
    hext_ref[HALO + ts:, :] = jnp.where(j < tiles_per_seq - 1, hn, jnp.zeros_like(hn))
    o_ref[...] = xc

    def body(c, carry):
        gate = jnp.dot(hext_ref[...], wg_ref[c], preferred_element_type=F32)
        val = jnp.dot(hext_ref[HALO:HALO + ts, :], wv_ref[c], preferred_element_type=F32)
        cw = cw_ref[c]
        rows = gate.shape[0]
        g_prev = pltpu.roll(gate, 1, 0)[HALO:HALO + ts]
        g_next = pltpu.roll(gate, rows - 1, 0)[HALO:HALO + ts]
        g_cur = gate[HALO:HALO + ts]
        conv = cb_ref[c] + cw[0:1] * g_prev + cw[1:2] * g_cur + cw[2:3] * g_next
        act = conv * (1.0 / (1.0 + jnp.exp(-conv))) * val
        o_ref[...] += jnp.dot(act.astype(BF16), wd_ref[c], preferred_element_type=F32)
        return carry

    lax.fori_loop(0, FF_CHUNKS, body, 0, unroll=True)


def _ffn(xf, g, wg, wv, cw, cb, wd, *, seq, ts):
    t = xf.shape[0]
    tiles_per_seq = seq // ts
    halo_blocks = ts // HALO
    last_halo = t // HALO - 1
    const3 = lambda shape: pl.BlockSpec(shape, lambda i: (0, 0, 0), pipeline_mode=pl.Buffered(1))
    return pl.pallas_call(
        functools.partial(_ffn_kernel, ts=ts, tiles_per_seq=tiles_per_seq),
        grid=(t // ts,),
        in_specs=[
            pl.BlockSpec((HALO, D_MODEL), lambda i: (jnp.maximum(i * halo_blocks - 1, 0), 0)),
            pl.BlockSpec((ts, D_MODEL), lambda i: (i, 0)),
            pl.BlockSpec((HALO, D_MODEL), lambda i: (jnp.minimum((i + 1) * halo_blocks, last_halo), 0)),
            pl.BlockSpec((1, D_MODEL), lambda i: (0, 0)),
            const3((FF_CHUNKS, D_MODEL, CHUNK)),
            const3((FF_CHUNKS, D_MODEL, CHUNK)),
            const3((FF_CHUNKS, 3, CHUNK)),
            const3((FF_CHUNKS, 1, CHUNK)),
            const3((FF_CHUNKS, CHUNK, D_MODEL)),
        ],
        out_specs=pl.BlockSpec((ts, D_MODEL), lambda i: (i, 0)),
        out_shape=jax.ShapeDtypeStruct((t, D_MODEL), F32),
        scratch_shapes=[pltpu.VMEM((ts + 2 * HALO, D_MODEL), BF16)],
        compiler_params=pltpu.CompilerParams(
            dimension_semantics=("parallel",), vmem_limit_bytes=48 * 1024 * 1024),
        name="convglu",
    )(xf, xf, xf, g, wg, wv, cw, cb, wd)


def _rope_tables(seq):
    inv = 1.0 / (ROPE_THETA ** (jnp.arange(0, HEAD_DIM, 2, dtype=F32) / HEAD_DIM))
    ang = jnp.arange(seq, dtype=F32)[:, None] * inv[None, :]
    cos, sin = jnp.cos(ang), jnp.sin(ang)
    reps = CHUNK // HEAD_DIM
    cos_t = jnp.tile(jnp.concatenate([cos, cos], axis=-1), (1, reps))
    sin_t = jnp.tile(jnp.concatenate([-sin, sin], axis=-1), (1, reps))
    return cos_t, sin_t


def _head_block_diag():
    idx = jnp.arange(CHUNK) // HEAD_DIM
    return (idx[:, None] == idx[None, :]).astype(BF16)


def kernel(x, g_attn, w_in, qn_a, kn_a, sink, qn_b, kn_b, lq1, lk1, lq2, lk2, subln, w_out, g_ffn,
           w_up, conv_w, conv_b, w_down):
    b, s, d = x.shape
    depth = w_in.shape[0]
    cos_t, sin_t = _rope_tables(s)
    bd = _head_block_diag()
    scale = HEAD_DIM ** -0.5 * LOG2E
    xf = x.reshape(b * s, d)
    o_q, o_k, o_v, o_qb, o_kb = SWA_Q, SWA_Q + SWA_KV, SWA_Q + 2 * SWA_KV, SWA_Q + 2 * SWA_KV + DIFF_Q, \
        SWA_Q + 2 * SWA_KV + 2 * DIFF_Q

    def dup_heads(w):
        return jnp.concatenate([w[:, :HEAD_DIM], w[:, :HEAD_DIM], w[:, HEAD_DIM:], w[:, HEAD_DIM:]], axis=1)

    for l in range(depth):
        lambda_init = 0.8 - 0.6 * math.exp(-0.3 * l)
        wl = w_in[l]
        w_proj = jnp.concatenate(
            [wl[:, :o_q], dup_heads(wl[:, o_q:o_k]), dup_heads(wl[:, o_k:o_v]), wl[:, o_v:]],
            axis=1).astype(BF16)
        gcol = jnp.concatenate([
            jnp.tile(qn_a[l], SWA_Q_HEADS) * scale, jnp.tile(kn_a[l], 2 * SWA_KV_HEADS),
            jnp.ones((2 * SWA_KV,), F32),
            jnp.tile(qn_b[l], 2 * DIFF_HEADS) * scale, jnp.tile(kn_b[l], 2 * DIFF_HEADS),
            jnp.ones((DIFF_V,), F32)])[None, :]
        proj = _inproj(xf, g_attn[l][None, :], w_proj, gcol, cos_t, sin_t, bd, seq=s, tm=512)
        proj3 = proj.reshape(b, s, PROJ_COLS)
        ya = _swa(proj3, sink[l])
        yb = _diff(proj3, lq1[l][None, :], lk1[l][None, :], lq2[l][None, :], lk2[l][None, :],
                   subln[l][None, :], lambda_init=lambda_init, tq=256)
        wo = w_out[l].astype(BF16)
        xf = _outproj(xf, ya.reshape(b * s, SWA_Q), yb.reshape(b * s, DIFF_V),
                      wo[:SWA_Q], wo[SWA_Q:], tm=512)
        wu = w_up[l].astype(BF16)
        wg = wu[:, :D_FF].reshape(d, FF_CHUNKS, CHUNK).transpose(1, 0, 2)
        wv = wu[:, D_FF:].reshape(d, FF_CHUNKS, CHUNK).transpose(1, 0, 2)
        cw = conv_w[l].reshape(3, FF_CHUNKS, CHUNK).transpose(1, 0, 2)
        cb = conv_b[l].reshape(FF_CHUNKS, 1, CHUNK)
        wd = w_down[l].astype(BF16).reshape(FF_CHUNKS, CHUNK, d)
        xf = _ffn(xf, g_ffn[l][None, :], wg, wv, cw, cb, wd, seq=s, ts=512)
    return xf.reshape(b, s, d)
```

```python
import functools
import math

import jax
import jax.numpy as jnp
from jax import lax
from jax.experimental import pallas as pl
from jax.experimental.pallas import tpu as pltpu

D_MODEL = 1024
HEAD_DIM = 64
SWA_Q_HEADS = 8
SWA_KV_HEADS = 2
WINDOW = 128
BLOCK = 128
DIFF_HEADS = 4
DIFF_V_DIM = 128
D_FF = 2816
ROPE_THETA = 10000.0
EPS = 1e-6
NEG = -1e30
LOG2E = 1.4426950408889634

SWA_Q = SWA_Q_HEADS * HEAD_DIM
SWA_KV = SWA_KV_HEADS * HEAD_DIM
DIFF_Q = DIFF_HEADS * 2 * HEAD_DIM
DIFF_V = DIFF_HEADS * DIFF_V_DIM

LANES = 128
MXU_COLS = 256
CHUNK = MXU_COLS
PROJ_COLS = 2560
N_PROJ_CHUNKS = PROJ_COLS // CHUNK
NORM_CHUNKS = (0, 1, 2, 4, 5, 6, 7)
FF_CHUNKS = D_FF // CHUNK
HALO = 16

BF16 = jnp.bfloat16
F32 = jnp.float32


def _rms(x):
    return lax.rsqrt(jnp.mean(x * x, axis=-1, keepdims=True) + EPS)


def _inproj_kernel(x_ref, g_ref, w_ref, gcol_ref, cos_ref, sin_ref, bd_ref, o_ref):
    x = x_ref[...]
    h = (x * _rms(x) * g_ref[...]).astype(BF16)
    cos = cos_ref[...]
    sin = sin_ref[...]
    lane = lax.broadcasted_iota(jnp.int32, cos.shape, 1)
    first_half = (lane % HEAD_DIM) < (HEAD_DIM // 2)

    def project(c):
        return jnp.dot(h, w_ref[:, c * CHUNK:(c + 1) * CHUNK], preferred_element_type=F32)

    p_next = project(0)
    for c in range(N_PROJ_CHUNKS):
        cols = slice(c * CHUNK, (c + 1) * CHUNK)
        p = p_next
        if c + 1 < N_PROJ_CHUNKS:
            p_next = project(c + 1)
        if c in NORM_CHUNKS:
            ss = jnp.dot((p * p).astype(BF16), bd_ref[...], preferred_element_type=F32)
            y = p * lax.rsqrt(ss * (1.0 / HEAD_DIM) + EPS) * gcol_ref[:, cols]
            partner = jnp.where(first_half,
                                pltpu.roll(y, CHUNK - HEAD_DIM // 2, 1),
                                pltpu.roll(y, HEAD_DIM // 2, 1))
            p = y * cos + partner * sin
        o_ref[:, cols] = p.astype(BF16)


def _inproj(xf, g, w, gcol, cos_t, sin_t, bd, *, seq, tm):
    t = xf.shape[0]
    tiles_per_seq = seq // tm
    return pl.pallas_call(
        _inproj_kernel,
        grid=(t // tm,),
        in_specs=[
            pl.BlockSpec((tm, D_MODEL), lambda i: (i, 0)),
            pl.BlockSpec((1, D_MODEL), lambda i: (0, 0)),
            pl.BlockSpec((D_MODEL, PROJ_COLS), lambda i: (0, 0)),
            pl.BlockSpec((1, PROJ_COLS), lambda i: (0, 0)),
            pl.BlockSpec((tm, CHUNK), lambda i: (i % tiles_per_seq, 0)),
            pl.BlockSpec((tm, CHUNK), lambda i: (i % tiles_per_seq, 0)),
            pl.BlockSpec((CHUNK, CHUNK), lambda i: (0, 0)),
        ],
        out_specs=pl.BlockSpec((tm, PROJ_COLS), lambda i: (i, 0)),
        out_shape=jax.ShapeDtypeStruct((t, PROJ_COLS), BF16),
        compiler_params=pltpu.CompilerParams(
            dimension_semantics=("parallel",), vmem_limit_bytes=48 * 1024 * 1024),
        name="inproj",
    )(xf, g, w, gcol, cos_t, sin_t, bd)


def _swa_kernel(sink_ref, q_ref, kp_ref, kc_ref, kn_ref, vp_ref, vc_ref, vn_ref, o_ref, *, ntiles, tq):
    n = pl.program_id(1)
    band = 3 * BLOCK
    blocks = tq // BLOCK
    k_ref_rows = [kp_ref] + [kc_ref] * blocks + [kn_ref]
    v_ref_rows = [vp_ref] + [vc_ref] * blocks + [vn_ref]

    def band_rows(refs, blk, lanes):
        parts = []
        for j in range(blk, blk + 3):
            ref = refs[j]
            parts.append(ref[:, lanes] if j in (0, blocks + 1)
                         else ref[(j - 1) * BLOCK:j * BLOCK, lanes])
        return jnp.concatenate(parts, axis=0)

    row = lax.broadcasted_iota(jnp.int32, (BLOCK, band), 0)
    col = lax.broadcasted_iota(jnp.int32, (BLOCK, band), 1)
    in_window = jnp.abs(col - BLOCK - row) <= WINDOW
    lo = lax.broadcasted_iota(jnp.int32, (BLOCK, LANES), 1) < HEAD_DIM
    hi = jnp.logical_not(lo)
    heads_per_kv = SWA_Q_HEADS // SWA_KV_HEADS
    ones = jnp.ones((band, LANES), BF16)
    zero = jnp.zeros((BLOCK, LANES), BF16)
    units = [(blk, kv) for blk in range(blocks) for kv in range(SWA_KV_HEADS)]

    def scores(unit):
        blk, kv = unit
        stack = []
        for hh in range(heads_per_kv):
            c0 = kv * heads_per_kv * HEAD_DIM + (hh // 2) * LANES
            qh = q_ref[blk * BLOCK:(blk + 1) * BLOCK, c0:c0 + LANES]
            stack.append(jnp.where(lo if hh % 2 == 0 else hi, qh, zero))
        qs = jnp.concatenate(stack, axis=0)
        kk = band_rows(k_ref_rows, blk, slice(kv * LANES, (kv + 1) * LANES))
        return lax.dot_general(qs, kk, (((1,), (1,)), ((), ())), preferred_element_type=F32)

    s_next = scores(units[0])
    for idx, unit in enumerate(units):
        blk, kv = unit
        s = s_next
        if idx + 1 < len(units):
            s_next = scores(units[idx + 1])
        valid = in_window
        if blk == 0:
            valid = valid & ((col >= BLOCK) | (n > 0))
        if blk == blocks - 1:
            valid = valid & ((col < 2 * BLOCK) | (n < ntiles - 1))
        vv = band_rows(v_ref_rows, blk, slice(kv * LANES, (kv + 1) * LANES))
        v1 = jnp.concatenate([vv, ones], axis=1)
        s = jnp.where(valid[None], s.reshape(heads_per_kv, BLOCK, band), NEG)
        s = s.reshape(heads_per_kv * BLOCK, band)
        sk = jnp.concatenate(
            [jnp.full((BLOCK, LANES), sink_ref[kv * heads_per_kv + hh] * LOG2E, F32)
             for hh in range(heads_per_kv)], axis=0)
        m = jnp.maximum(jnp.max(s, axis=-1, keepdims=True), sk)
        p = jnp.concatenate(
            [jnp.exp2(s[:, j * LANES:(j + 1) * LANES] - m) for j in range(band // LANES)],
            axis=1).astype(BF16)
        o2 = jnp.dot(p, v1, preferred_element_type=F32)
        denom = o2[:, LANES:] + jnp.exp2(sk - m)
        o = o2[:, :LANES] / denom
        for pair in range(heads_per_kv // 2):
            out = jnp.where(lo, o[2 * pair * BLOCK:(2 * pair + 1) * BLOCK],
                            o[(2 * pair + 1) * BLOCK:(2 * pair + 2) * BLOCK])
            c0 = kv * heads_per_kv * HEAD_DIM + pair * LANES
            o_ref[blk * BLOCK:(blk + 1) * BLOCK, c0:c0 + LANES] = out.astype(BF16)


def _swa(proj3, sink, *, tq):
    b, s, _ = proj3.shape
    ntiles = s // tq
    per_tile = tq // BLOCK
    last_block = s // BLOCK - 1
    kcol = SWA_Q // CHUNK
    vcol = kcol + 1
    prev = lambda col: pl.BlockSpec(
        (None, BLOCK, CHUNK), lambda bi, n: (bi, jnp.maximum(n * per_tile - 1, 0), col))
    own = lambda col: pl.BlockSpec((None, tq, CHUNK), lambda bi, n: (bi, n, col))
    nxt = lambda col: pl.BlockSpec(
        (None, BLOCK, CHUNK), lambda bi, n: (bi, jnp.minimum((n + 1) * per_tile, last_block), col))
    return pl.pallas_call(
        functools.partial(_swa_kernel, ntiles=ntiles, tq=tq),
        grid=(b, ntiles),
        in_specs=[
            pl.BlockSpec(memory_space=pltpu.SMEM),
            pl.BlockSpec((None, tq, SWA_Q), lambda bi, n: (bi, n, 0)),
            prev(kcol), own(kcol), nxt(kcol),
            prev(vcol), own(vcol), nxt(vcol),
        ],
        out_specs=pl.BlockSpec((None, tq, SWA_Q), lambda bi, n: (bi, n, 0)),
        out_shape=jax.ShapeDtypeStruct((b, s, SWA_Q), BF16),
        compiler_params=pltpu.CompilerParams(dimension_semantics=("parallel", "parallel")),
        name="swa",
    )(sink, proj3, proj3, proj3, proj3, proj3, proj3, proj3)


def _diff_kernel(lq1_ref, lk1_ref, lq2_ref, lk2_ref, subln_ref, q_ref, k_ref, v_ref, o_ref, v1_ref, *,
                 lambda_init, tq, rows):
    lam = (jnp.exp(jnp.sum(lq1_ref[...] * lk1_ref[...], axis=-1, keepdims=True))
           - jnp.exp(jnp.sum(lq2_ref[...] * lk2_ref[...], axis=-1, keepdims=True))
           + lambda_init)

    @pl.when(pl.program_id(1) == 0)
    def _():
        for h in range(DIFF_HEADS):
            v1_ref[h, :, :DIFF_V_DIM] = v_ref[:, h * DIFF_V_DIM:(h + 1) * DIFF_V_DIM]
            v1_ref[h, :, DIFF_V_DIM:] = jnp.ones((v_ref.shape[0], DIFF_V_DIM), BF16)

    lo = lax.broadcasted_iota(jnp.int32, (rows, LANES), 1) < HEAD_DIM
    hi = jnp.logical_not(lo)
    zero = jnp.zeros((rows, LANES), BF16)
    units = [(h, r, c) for h in range(DIFF_HEADS) for r in range(tq // rows) for c in range(2)]

    def scores(unit):
        h, r, c = unit
        qh = q_ref[r * rows:(r + 1) * rows, h * LANES:(h + 1) * LANES]
        qm = jnp.where(lo if c == 0 else hi, qh, zero)
        return lax.dot_general(qm, k_ref[:, h * LANES:(h + 1) * LANES], (((1,), (1,)), ((), ())),
                               preferred_element_type=F32)

    def attend(s, h):
        m = jnp.max(s, axis=-1, keepdims=True)
        p = jnp.exp2(s - m).astype(BF16)
        o2 = jnp.dot(p, v1_ref[h], preferred_element_type=F32)
        return o2[:, :DIFF_V_DIM] / o2[:, DIFF_V_DIM:]

    s_next = scores(units[0])
    first = None
    for idx, unit in enumerate(units):
        h, r, c = unit
        s = s_next
        if idx + 1 < len(units):
            s_next = scores(units[idx + 1])
        o = attend(s, h)
        if c == 0:
            first = o
        else:
            od = first - lam * o
            y = od * _rms(od) * subln_ref[...] * (1.0 - lambda_init)
            o_ref[r * rows:(r + 1) * rows, h * DIFF_V_DIM:(h + 1) * DIFF_V_DIM] = y.astype(BF16)


def _diff(proj3, lq1, lk1, lq2, lk2, subln, *, lambda_init, tq, rows):
    b, s, _ = proj3.shape
    qcol = (2 * SWA_Q) // DIFF_Q
    small = lambda w: pl.BlockSpec((1, w), lambda bi, i: (0, 0))
    return pl.pallas_call(
        functools.partial(_diff_kernel, lambda_init=lambda_init, tq=tq, rows=rows),
        grid=(b, s // tq),
        in_specs=[
            small(HEAD_DIM), small(HEAD_DIM), small(HEAD_DIM), small(HEAD_DIM), small(DIFF_V_DIM),
            pl.BlockSpec((None, tq, DIFF_Q), lambda bi, i: (bi, i, qcol)),
            pl.BlockSpec((None, s, DIFF_Q), lambda bi, i: (bi, 0, qcol + 1)),
            pl.BlockSpec((None, s, DIFF_V), lambda bi, i: (bi, 0, qcol + 2)),
        ],
        out_specs=pl.BlockSpec((None, tq, DIFF_V), lambda bi, i: (bi, i, 0)),
        out_shape=jax.ShapeDtypeStruct((b, s, DIFF_V), BF16),
        scratch_shapes=[pltpu.VMEM((DIFF_HEADS, s, 2 * DIFF_V_DIM), BF16)],
        compiler_params=pltpu.CompilerParams(
            dimension_semantics=("parallel", "arbitrary"),
            vmem_limit_bytes=48 * 1024 * 1024),
        name="diffattn",
    )(lq1, lk1, lq2, lk2, subln, proj3, proj3, proj3)


def _outproj_kernel(x_ref, ya_ref, yb_ref, wa_ref, wb_ref, o_ref):
    acc = x_ref[...] + jnp.dot(ya_ref[...], wa_ref[...], preferred_element_type=F32)
    o_ref[...] = acc + jnp.dot(yb_ref[...], wb_ref[...], preferred_element_type=F32)


def _outproj(xf, ya, yb, wa, wb, *, tm):
    t = xf.shape[0]
    return pl.pallas_call(
        _outproj_kernel,
        grid=(t // tm,),
        in_specs=[
            pl.BlockSpec((tm, D_MODEL), lambda i: (i, 0)),
            pl.BlockSpec((tm, SWA_Q), lambda i: (i, 0)),
            pl.BlockSpec((tm, DIFF_V), lambda i: (i, 0)),
            pl.BlockSpec((SWA_Q, D_MODEL), lambda i: (0, 0)),
            pl.BlockSpec((DIFF_V, D_MODEL), lambda i: (0, 0)),
        ],
        out_specs=pl.BlockSpec((tm, D_MODEL), lambda i: (i, 0)),
        out_shape=jax.ShapeDtypeStruct((t, D_MODEL), F32),
        compiler_params=pltpu.CompilerParams(dimension_semantics=("parallel",)),
        name="outproj",
    )(xf, ya, yb, wa, wb)


def _ffn_kernel(xp_ref, xc_ref, xn_ref, g_ref, wg_ref, wv_ref, cw_ref, cb_ref, wd_ref, o_ref,
                hext_ref, *, ts, tiles_per_seq):
    j = pl.program_id(0) % tiles_per_seq
    g = g_ref[...]

    def norm(xx):
        return (xx * _rms(xx) * g).astype(BF16)

    hp = norm(xp_ref[...])
    hn = norm(xn_ref[...])
    hext_ref[0:HALO, :] = jnp.where(j > 0, hp, jnp.zeros_like(hp))
    xc = xc_ref[...]
    hext_ref[HALO:HALO + ts, :] = norm(xc)
    hext_ref[HALO + ts:, :] = jnp.where(j < tiles_per_seq - 1, hn, jnp.zeros_like(hn))

    def up(c):
        gate = jnp.dot(hext_ref[...], wg_ref[c], preferred_element_type=F32)
        val = jnp.dot(hext_ref[HALO:HALO + ts, :], wv_ref[c], preferred_element_type=F32)
        return gate, val

    ahead = up(0)
    acc = xc
    for c in range(FF_CHUNKS):
        gate, val = ahead
        if c + 1 < FF_CHUNKS:
            ahead = up(c + 1)
        cw = cw_ref[c]
        rows = gate.shape[0]
        g_prev = pltpu.roll(gate, 1, 0)[HALO:HALO + ts]
        g_next = pltpu.roll(gate, rows - 1, 0)[HALO:HALO + ts]
        g_cur = gate[HALO:HALO + ts]
        conv = cb_ref[c] + cw[0:1] * g_prev + cw[1:2] * g_cur + cw[2:3] * g_next
        act = conv * (1.0 / (1.0 + jnp.exp(-conv))) * val
        acc = acc + jnp.dot(act.astype(BF16), wd_ref[c], preferred_element_type=F32)
    o_ref[...] = acc


def _ffn(xf, g, wg, wv, cw, cb, wd, *, seq, ts):
    t = xf.shape[0]
    tiles_per_seq = seq // ts
    halo_blocks = ts // HALO
    last_halo = t // HALO - 1
    const3 = lambda shape: pl.BlockSpec(shape, lambda i: (0, 0, 0), pipeline_mode=pl.Buffered(1))
    return pl.pallas_call(
        functools.partial(_ffn_kernel, ts=ts, tiles_per_seq=tiles_per_seq),
        grid=(t // ts,),
        in_specs=[
            pl.BlockSpec((HALO, D_MODEL), lambda i: (jnp.maximum(i * halo_blocks - 1, 0), 0)),
            pl.BlockSpec((ts, D_MODEL), lambda i: (i, 0)),
            pl.BlockSpec((HALO, D_MODEL), lambda i: (jnp.minimum((i + 1) * halo_blocks, last_halo), 0)),
            pl.BlockSpec((1, D_MODEL), lambda i: (0, 0)),
            const3((FF_CHUNKS, D_MODEL, CHUNK)),
            const3((FF_CHUNKS, D_MODEL, CHUNK)),
            const3((FF_CHUNKS, 3, CHUNK)),
            const3((FF_CHUNKS, 1, CHUNK)),
            const3((FF_CHUNKS, CHUNK, D_MODEL)),
        ],
        out_specs=pl.BlockSpec((ts, D_MODEL), lambda i: (i, 0)),
        out_shape=jax.ShapeDtypeStruct((t, D_MODEL), F32),
        scratch_shapes=[pltpu.VMEM((ts + 2 * HALO, D_MODEL), BF16)],
        compiler_params=pltpu.CompilerParams(
            dimension_semantics=("parallel",), vmem_limit_bytes=48 * 1024 * 1024),
        name="convglu",
    )(xf, xf, xf, g, wg, wv, cw, cb, wd)


def _rope_tables(seq):
    inv = 1.0 / (ROPE_THETA ** (jnp.arange(0, HEAD_DIM, 2, dtype=F32) / HEAD_DIM))
    ang = jnp.arange(seq, dtype=F32)[:, None] * inv[None, :]
    cos, sin = jnp.cos(ang), jnp.sin(ang)
    reps = CHUNK // HEAD_DIM
    cos_t = jnp.tile(jnp.concatenate([cos, cos], axis=-1), (1, reps))
    sin_t = jnp.tile(jnp.concatenate([-sin, sin], axis=-1), (1, reps))
    return cos_t, sin_t


def _head_block_diag():
    idx = jnp.arange(CHUNK) // HEAD_DIM
    return (idx[:, None] == idx[None, :]).astype(BF16)


def kernel(x, g_attn, w_in, qn_a, kn_a, sink, qn_b, kn_b, lq1, lk1, lq2, lk2, subln, w_out, g_ffn,
           w_up, conv_w, conv_b, w_down):
    b, s, d = x.shape
    depth = w_in.shape[0]
    cos_t, sin_t = _rope_tables(s)
    bd = _head_block_diag()
    scale = HEAD_DIM ** -0.5 * LOG2E
    xf = x.reshape(b * s, d)
    o_q, o_k, o_v, o_qb, o_kb = SWA_Q, SWA_Q + SWA_KV, SWA_Q + 2 * SWA_KV, SWA_Q + 2 * SWA_KV + DIFF_Q, \
        SWA_Q + 2 * SWA_KV + 2 * DIFF_Q

    def dup_heads(w):
        return jnp.concatenate([w[:, :HEAD_DIM], w[:, :HEAD_DIM], w[:, HEAD_DIM:], w[:, HEAD_DIM:]], axis=1)

    for l in range(depth):
        lambda_init = 0.8 - 0.6 * math.exp(-0.3 * l)
        wl = w_in[l]
        w_proj = jnp.concatenate(
            [wl[:, :o_q], dup_heads(wl[:, o_q:o_k]), dup_heads(wl[:, o_k:o_v]), wl[:, o_v:]],
            axis=1).astype(BF16)
        gcol = jnp.concatenate([
            jnp.tile(qn_a[l], SWA_Q_HEADS) * scale, jnp.tile(kn_a[l], 2 * SWA_KV_HEADS),
            jnp.ones((2 * SWA_KV,), F32),
            jnp.tile(qn_b[l], 2 * DIFF_HEADS) * scale, jnp.tile(kn_b[l], 2 * DIFF_HEADS),
            jnp.ones((DIFF_V,), F32)])[None, :]
        proj = _inproj(xf, g_attn[l][None, :], w_proj, gcol, cos_t, sin_t, bd, seq=s, tm=512)
        proj3 = proj.reshape(b, s, PROJ_COLS)
        ya = _swa(proj3, sink[l], tq=512)
        yb = _diff(proj3, lq1[l][None, :], lk1[l][None, :], lq2[l][None, :], lk2[l][None, :],
                   subln[l][None, :], lambda_init=lambda_init, tq=512, rows=256)
        wo = w_out[l].astype(BF16)
        xf = _outproj(xf, ya.reshape(b * s, SWA_Q), yb.reshape(b * s, DIFF_V),
                      wo[:SWA_Q], wo[SWA_Q:], tm=512)
        wu = w_up[l].astype(BF16)
        wg = wu[:, :D_FF].reshape(d, FF_CHUNKS, CHUNK).transpose(1, 0, 2)
        wv = wu[:, D_FF:].reshape(d, FF_CHUNKS, CHUNK).transpose(1, 0, 2)
        cw = conv_w[l].reshape(3, FF_CHUNKS, CHUNK).transpose(1, 0, 2)
        cb = conv_b[l].reshape(FF_CHUNKS, 1, CHUNK)
        wd = w_down[l].astype(BF16).reshape(FF_CHUNKS, CHUNK, d)
        xf = _ffn(xf, g_ffn[l][None, :], wg, wv, cw, cb, wd, seq=s, ts=512)
    return xf.reshape(b, s, d)
```

```python
import functools
import math

import jax
import jax.numpy as jnp
import numpy as np
from jax import lax
from jax.experimental import pallas as pl
from jax.experimental.pallas import tpu as pltpu

D_MODEL = 1024
HEAD_DIM = 64
SWA_Q_HEADS = 8
SWA_KV_HEADS = 2
WINDOW = 128
BLOCK = 128
DIFF_HEADS = 4
DIFF_V_DIM = 128
D_FF = 2816
ROPE_THETA = 10000.0
EPS = 1e-6
NEG = -1e30
LOG2E = 1.4426950408889634

SWA_Q = SWA_Q_HEADS * HEAD_DIM
SWA_KV = SWA_KV_HEADS * HEAD_DIM
DIFF_Q = DIFF_HEADS * 2 * HEAD_DIM
DIFF_V = DIFF_HEADS * DIFF_V_DIM

LANES = 128
MXU_COLS = 256
CHUNK = MXU_COLS
PROJ_COLS = 2560
N_PROJ_CHUNKS = PROJ_COLS // CHUNK
NORM_CHUNKS = (0, 1, 2, 4, 5, 6, 7)
FF_CHUNKS = D_FF // CHUNK
HALO = 16

BF16 = jnp.bfloat16
F32 = jnp.float32


def _rms(x):
    return lax.rsqrt(jnp.mean(x * x, axis=-1, keepdims=True) + EPS)


def _layer(shape, l, **kwargs):
    zeros = (0,) * len(shape)
    return pl.BlockSpec((None,) + tuple(shape), lambda *_: (l,) + zeros, **kwargs)


def _inproj_kernel(x_ref, g_ref, w_ref, gcol_ref, cos_ref, sin_ref, bd_ref, o_ref):
    x = x_ref[...]
    h = (x * _rms(x) * g_ref[...]).astype(BF16)
    cos = cos_ref[...]
    sin = sin_ref[...]
    lane = lax.broadcasted_iota(jnp.int32, cos.shape, 1)
    first_half = (lane % HEAD_DIM) < (HEAD_DIM // 2)

    def project(c):
        return jnp.dot(h, w_ref[:, c * CHUNK:(c + 1) * CHUNK], preferred_element_type=F32)

    p_next = project(0)
    for c in range(N_PROJ_CHUNKS):
        cols = slice(c * CHUNK, (c + 1) * CHUNK)
        p = p_next
        if c + 1 < N_PROJ_CHUNKS:
            p_next = project(c + 1)
        if c in NORM_CHUNKS:
            ss = jnp.dot((p * p).astype(BF16), bd_ref[...], preferred_element_type=F32)
            y = p * lax.rsqrt(ss * (1.0 / HEAD_DIM) + EPS) * gcol_ref[:, cols]
            partner = jnp.where(first_half,
                                pltpu.roll(y, CHUNK - HEAD_DIM // 2, 1),
                                pltpu.roll(y, HEAD_DIM // 2, 1))
            p = y * cos + partner * sin
        o_ref[:, cols] = p.astype(BF16)


def _inproj(xf, g, w, gcol, cos_t, sin_t, bd, *, layer, seq, tm):
    t = xf.shape[0]
    tiles_per_seq = seq // tm
    return pl.pallas_call(
        _inproj_kernel,
        grid=(t // tm,),
        in_specs=[
            pl.BlockSpec((tm, D_MODEL), lambda i: (i, 0)),
            _layer((1, D_MODEL), layer),
            _layer((D_MODEL, PROJ_COLS), layer),
            _layer((1, PROJ_COLS), layer),
            pl.BlockSpec((tm, CHUNK), lambda i: (i % tiles_per_seq, 0)),
            pl.BlockSpec((tm, CHUNK), lambda i: (i % tiles_per_seq, 0)),
            pl.BlockSpec((CHUNK, CHUNK), lambda i: (0, 0)),
        ],
        out_specs=pl.BlockSpec((tm, PROJ_COLS), lambda i: (i, 0)),
        out_shape=jax.ShapeDtypeStruct((t, PROJ_COLS), BF16),
        compiler_params=pltpu.CompilerParams(
            dimension_semantics=("parallel",), vmem_limit_bytes=48 * 1024 * 1024),
        name="inproj",
    )(xf, g, w, gcol, cos_t, sin_t, bd)


def _swa_kernel(sink_ref, q_ref, kp_ref, kc_ref, kn_ref, vp_ref, vc_ref, vn_ref, o_ref, *,
                layer, ntiles, tq):
    n = pl.program_id(1)
    band = 3 * BLOCK
    blocks = tq // BLOCK
    k_ref_rows = [kp_ref] + [kc_ref] * blocks + [kn_ref]
    v_ref_rows = [vp_ref] + [vc_ref] * blocks + [vn_ref]

    def band_rows(refs, blk, lanes):
        parts = []
        for j in range(blk, blk + 3):
            ref = refs[j]
            parts.append(ref[:, lanes] if j in (0, blocks + 1)
                         else ref[(j - 1) * BLOCK:j * BLOCK, lanes])
        return jnp.concatenate(parts, axis=0)

    row = lax.broadcasted_iota(jnp.int32, (BLOCK, band), 0)
    col = lax.broadcasted_iota(jnp.int32, (BLOCK, band), 1)
    in_window = jnp.abs(col - BLOCK - row) <= WINDOW
    lo = lax.broadcasted_iota(jnp.int32, (BLOCK, LANES), 1) < HEAD_DIM
    hi = jnp.logical_not(lo)
    heads_per_kv = SWA_Q_HEADS // SWA_KV_HEADS
    ones = jnp.ones((band, LANES), BF16)
    zero = jnp.zeros((BLOCK, LANES), BF16)
    units = [(blk, kv) for blk in range(blocks) for kv in range(SWA_KV_HEADS)]

    def scores(unit):
        blk, kv = unit
        stack = []
        for hh in range(heads_per_kv):
            c0 = kv * heads_per_kv * HEAD_DIM + (hh // 2) * LANES
            qh = q_ref[blk * BLOCK:(blk + 1) * BLOCK, c0:c0 + LANES]
            stack.append(jnp.where(lo if hh % 2 == 0 else hi, qh, zero))
        qs = jnp.concatenate(stack, axis=0)
        kk = band_rows(k_ref_rows, blk, slice(kv * LANES, (kv + 1) * LANES))
        return lax.dot_general(qs, kk, (((1,), (1,)), ((), ())), preferred_element_type=F32)

    s_next = scores(units[0])
    for idx, unit in enumerate(units):
        blk, kv = unit
        s = s_next
        if idx + 1 < len(units):
            s_next = scores(units[idx + 1])
        valid = in_window
        if blk == 0:
            valid = valid & ((col >= BLOCK) | (n > 0))
        if blk == blocks - 1:
            valid = valid & ((col < 2 * BLOCK) | (n < ntiles - 1))
        vv = band_rows(v_ref_rows, blk, slice(kv * LANES, (kv + 1) * LANES))
        v1 = jnp.concatenate([vv, ones], axis=1)
        s = jnp.where(valid[None], s.reshape(heads_per_kv, BLOCK, band), NEG)
        s = s.reshape(heads_per_kv * BLOCK, band)
        sk = jnp.concatenate(
            [jnp.full((BLOCK, LANES), sink_ref[layer, kv * heads_per_kv + hh] * LOG2E, F32)
             for hh in range(heads_per_kv)], axis=0)
        m = jnp.maximum(jnp.max(s, axis=-1, keepdims=True), sk)
        p = jnp.concatenate(
            [jnp.exp2(s[:, j * LANES:(j + 1) * LANES] - m) for j in range(band // LANES)],
            axis=1).astype(BF16)
        o2 = jnp.dot(p, v1, preferred_element_type=F32)
        denom = o2[:, LANES:] + jnp.exp2(sk - m)
        o = o2[:, :LANES] / denom
        for pair in range(heads_per_kv // 2):
            out = jnp.where(lo, o[2 * pair * BLOCK:(2 * pair + 1) * BLOCK],
                            o[(2 * pair + 1) * BLOCK:(2 * pair + 2) * BLOCK])
            c0 = kv * heads_per_kv * HEAD_DIM + pair * LANES
            o_ref[blk * BLOCK:(blk + 1) * BLOCK, c0:c0 + LANES] = out.astype(BF16)


def _swa(proj3, sink, *, layer, tq):
    b, s, _ = proj3.shape
    ntiles = s // tq
    per_tile = tq // BLOCK
    last_block = s // BLOCK - 1
    kcol = SWA_Q // CHUNK
    vcol = kcol + 1
    prev = lambda col: pl.BlockSpec(
        (None, BLOCK, CHUNK), lambda bi, n: (bi, jnp.maximum(n * per_tile - 1, 0), col))
    own = lambda col: pl.BlockSpec((None, tq, CHUNK), lambda bi, n: (bi, n, col))
    nxt = lambda col: pl.BlockSpec(
        (None, BLOCK, CHUNK), lambda bi, n: (bi, jnp.minimum((n + 1) * per_tile, last_block), col))
    return pl.pallas_call(
        functools.partial(_swa_kernel, layer=layer, ntiles=ntiles, tq=tq),
        grid=(b, ntiles),
        in_specs=[
            pl.BlockSpec(memory_space=pltpu.SMEM),
            pl.BlockSpec((None, tq, SWA_Q), lambda bi, n: (bi, n, 0)),
            prev(kcol), own(kcol), nxt(kcol),
            prev(vcol), own(vcol), nxt(vcol),
        ],
        out_specs=pl.BlockSpec((None, tq, SWA_Q), lambda bi, n: (bi, n, 0)),
        out_shape=jax.ShapeDtypeStruct((b, s, SWA_Q), BF16),
        compiler_params=pltpu.CompilerParams(dimension_semantics=("parallel", "parallel")),
        name="swa",
    )(sink, proj3, proj3, proj3, proj3, proj3, proj3, proj3)


def _diff_kernel(lq1_ref, lk1_ref, lq2_ref, lk2_ref, subln_ref, q_ref, k_ref, v_ref, x_ref, ya_ref,
                 wa_ref, wb_ref, o_ref, v1_ref, yb_ref, *, lambda_init, tq, rows):
    acc = x_ref[...] + jnp.dot(ya_ref[...], wa_ref[...], preferred_element_type=F32)
    lam = (jnp.exp(jnp.sum(lq1_ref[...] * lk1_ref[...], axis=-1, keepdims=True))
           - jnp.exp(jnp.sum(lq2_ref[...] * lk2_ref[...], axis=-1, keepdims=True))
           + lambda_init)

    @pl.when(pl.program_id(1) == 0)
    def _():
        for h in range(DIFF_HEADS):
            v1_ref[h, :, :DIFF_V_DIM] = v_ref[:, h * DIFF_V_DIM:(h + 1) * DIFF_V_DIM]
            v1_ref[h, :, DIFF_V_DIM:] = jnp.ones((v_ref.shape[0], DIFF_V_DIM), BF16)

    lo = lax.broadcasted_iota(jnp.int32, (rows, LANES), 1) < HEAD_DIM
    hi = jnp.logical_not(lo)
    zero = jnp.zeros((rows, LANES), BF16)
    units = [(h, r, c) for h in range(DIFF_HEADS) for r in range(tq // rows) for c in range(2)]

    def scores(unit):
        h, r, c = unit
        qh = q_ref[r * rows:(r + 1) * rows, h * LANES:(h + 1) * LANES]
        qm = jnp.where(lo if c == 0 else hi, qh, zero)
        return lax.dot_general(qm, k_ref[:, h * LANES:(h + 1) * LANES], (((1,), (1,)), ((), ())),
                               preferred_element_type=F32)

    def attend(s, h):
        m = jnp.max(s, axis=-1, keepdims=True)
        p = jnp.exp2(s - m).astype(BF16)
        o2 = jnp.dot(p, v1_ref[h], preferred_element_type=F32)
        return o2[:, :DIFF_V_DIM] / o2[:, DIFF_V_DIM:]

    s_next = scores(units[0])
    first = None
    for idx, unit in enumerate(units):
        h, r, c = unit
        s = s_next
        if idx + 1 < len(units):
            s_next = scores(units[idx + 1])
        o = attend(s, h)
        if c == 0:
            first = o
        else:
            od = first - lam * o
            y = od * _rms(od) * subln_ref[...] * (1.0 - lambda_init)
            yb_ref[r * rows:(r + 1) * rows, h * DIFF_V_DIM:(h + 1) * DIFF_V_DIM] = y.astype(BF16)
    o_ref[...] = acc + jnp.dot(yb_ref[...], wb_ref[...], preferred_element_type=F32)


def _diff_outproj(proj3, x3, ya, lq1, lk1, lq2, lk2, subln, wo, *, layer, lambda_init, tq, rows):
    b, s, _ = proj3.shape
    qcol = (2 * SWA_Q) // DIFF_Q
    return pl.pallas_call(
        functools.partial(_diff_kernel, lambda_init=lambda_init, tq=tq, rows=rows),
        grid=(b, s // tq),
        in_specs=[
            _layer((1, HEAD_DIM), layer), _layer((1, HEAD_DIM), layer),
            _layer((1, HEAD_DIM), layer), _layer((1, HEAD_DIM), layer),
            _layer((1, DIFF_V_DIM), layer),
            pl.BlockSpec((None, tq, DIFF_Q), lambda bi, i: (bi, i, qcol)),
            pl.BlockSpec((None, s, DIFF_Q), lambda bi, i: (bi, 0, qcol + 1)),
            pl.BlockSpec((None, s, DIFF_V), lambda bi, i: (bi, 0, qcol + 2)),
            pl.BlockSpec((None, tq, D_MODEL), lambda bi, i: (bi, i, 0)),
            pl.BlockSpec((None, tq, SWA_Q), lambda bi, i: (bi, i, 0)),
            pl.BlockSpec((None, SWA_Q, D_MODEL), lambda bi, i: (layer, 0, 0)),
            pl.BlockSpec((None, DIFF_V, D_MODEL), lambda bi, i: (layer, 1, 0)),
        ],
        out_specs=pl.BlockSpec((None, tq, D_MODEL), lambda bi, i: (bi, i, 0)),
        out_shape=jax.ShapeDtypeStruct((b, s, D_MODEL), F32),
        scratch_shapes=[pltpu.VMEM((DIFF_HEADS, s, 2 * DIFF_V_DIM), BF16),
                        pltpu.VMEM((tq, DIFF_V), BF16)],
        compiler_params=pltpu.CompilerParams(
            dimension_semantics=("parallel", "arbitrary"),
            vmem_limit_bytes=48 * 1024 * 1024),
        name="diffattn",
    )(lq1, lk1, lq2, lk2, subln, proj3, proj3, proj3, x3, ya, wo, wo)


def _ffn_kernel(xp_ref, xc_ref, xn_ref, g_ref, wu_ref, cw_ref, cb_ref, wd_ref, o_ref,
                hext_ref, *, ts, tiles_per_seq):
    j = pl.program_id(0) % tiles_per_seq
    g = g_ref[...]

    def norm(xx):
        return (xx * _rms(xx) * g).astype(BF16)

    hp = norm(xp_ref[...])
    hn = norm(xn_ref[...])
    hext_ref[0:HALO, :] = jnp.where(j > 0, hp, jnp.zeros_like(hp))
    xc = xc_ref[...]
    hext_ref[HALO:HALO + ts, :] = norm(xc)
    hext_ref[HALO + ts:, :] = jnp.where(j < tiles_per_seq - 1, hn, jnp.zeros_like(hn))

    def up(c):
        gate = jnp.dot(hext_ref[...], wu_ref[:, c * CHUNK:(c + 1) * CHUNK],
                       preferred_element_type=F32)
        val = jnp.dot(hext_ref[HALO:HALO + ts, :], wu_ref[:, D_FF + c * CHUNK:D_FF + (c + 1) * CHUNK],
                      preferred_element_type=F32)
        return gate, val

    ahead = up(0)
    acc = xc
    for c in range(FF_CHUNKS):
        gate, val = ahead
        if c + 1 < FF_CHUNKS:
            ahead = up(c + 1)
        cols = slice(c * CHUNK, (c + 1) * CHUNK)
        cw = cw_ref[:, cols]
        rows = gate.shape[0]
        g_prev = pltpu.roll(gate, 1, 0)[HALO:HALO + ts]
        g_next = pltpu.roll(gate, rows - 1, 0)[HALO:HALO + ts]
        g_cur = gate[HALO:HALO + ts]
        conv = cb_ref[:, cols] + cw[0:1] * g_prev + cw[1:2] * g_cur + cw[2:3] * g_next
        act = conv * (1.0 / (1.0 + jnp.exp(-conv))) * val
        acc = acc + jnp.dot(act.astype(BF16), wd_ref[cols, :], preferred_element_type=F32)
    o_ref[...] = acc


def _ffn(xf, g, wu, cw, cb, wd, *, layer, seq, ts):
    t = xf.shape[0]
    tiles_per_seq = seq // ts
    halo_blocks = ts // HALO
    last_halo = t // HALO - 1
    resident = pl.Buffered(1)
    return pl.pallas_call(
        functools.partial(_ffn_kernel, ts=ts, tiles_per_seq=tiles_per_seq),
        grid=(t // ts,),
        in_specs=[
            pl.BlockSpec((HALO, D_MODEL), lambda i: (jnp.maximum(i * halo_blocks - 1, 0), 0)),
            pl.BlockSpec((ts, D_MODEL), lambda i: (i, 0)),
            pl.BlockSpec((HALO, D_MODEL), lambda i: (jnp.minimum((i + 1) * halo_blocks, last_halo), 0)),
            _layer((1, D_MODEL), layer),
            _layer((D_MODEL, 2 * D_FF), layer, pipeline_mode=resident),
            _layer((3, D_FF), layer),
            _layer((1, D_FF), layer),
            _layer((D_FF, D_MODEL), layer, pipeline_mode=resident),
        ],
        out_specs=pl.BlockSpec((ts, D_MODEL), lambda i: (i, 0)),
        out_shape=jax.ShapeDtypeStruct((t, D_MODEL), F32),
        scratch_shapes=[pltpu.VMEM((ts + 2 * HALO, D_MODEL), BF16)],
        compiler_params=pltpu.CompilerParams(
            dimension_semantics=("parallel",), vmem_limit_bytes=56 * 1024 * 1024),
        name="convglu",
    )(xf, xf, xf, g, wu, cw, cb, wd)


def _rope_tables(seq):
    inv = 1.0 / (ROPE_THETA ** (np.arange(0, HEAD_DIM, 2, dtype=np.float64) / HEAD_DIM))
    ang = np.arange(seq, dtype=np.float64)[:, None] * inv[None, :]
    cos, sin = np.cos(ang), np.sin(ang)
    reps = CHUNK // HEAD_DIM
    cos_t = np.tile(np.concatenate([cos, cos], axis=-1), (1, reps))
    sin_t = np.tile(np.concatenate([-sin, sin], axis=-1), (1, reps))
    return jnp.asarray(cos_t, F32), jnp.asarray(sin_t, F32)


def _head_block_diag():
    idx = np.arange(CHUNK) // HEAD_DIM
    return jnp.asarray(idx[:, None] == idx[None, :], BF16)


def kernel(x, g_attn, w_in, qn_a, kn_a, sink, qn_b, kn_b, lq1, lk1, lq2, lk2, subln, w_out, g_ffn,
           w_up, conv_w, conv_b, w_down):
    b, s, d = x.shape
    depth = w_in.shape[0]
    cos_t, sin_t = _rope_tables(s)
    bd = _head_block_diag()
    scale = HEAD_DIM ** -0.5 * LOG2E
    xf = x.reshape(b * s, d)

    row = lambda p: p[:, None, :]
    heads = lambda g, n: jnp.tile(g, (1, n))
    kv_cols = [w_in[..., SWA_Q + j * HEAD_DIM:SWA_Q + (j + 1) * HEAD_DIM] for j in range(2 * SWA_KV_HEADS)]
    w_proj = jnp.concatenate(
        [w_in[..., :SWA_Q]] + [c for c in kv_cols for _ in range(2)] + [w_in[..., SWA_Q + 2 * SWA_KV:]],
        axis=-1).astype(BF16)
    gcol = row(jnp.concatenate([
        heads(qn_a, SWA_Q_HEADS) * scale, heads(kn_a, 2 * SWA_KV_HEADS),
        jnp.ones((depth, 2 * SWA_KV), F32),
        heads(qn_b, 2 * DIFF_HEADS) * scale, heads(kn_b, 2 * DIFF_HEADS),
        jnp.ones((depth, DIFF_V), F32)], axis=-1))
    wo = w_out.astype(BF16)
    wu = w_up.astype(BF16)
    wd = w_down.astype(BF16)
    g_attn3, g_ffn3, subln3, conv_b3 = row(g_attn), row(g_ffn), row(subln), row(conv_b)
    lq1_3, lk1_3, lq2_3, lk2_3 = row(lq1), row(lk1), row(lq2), row(lk2)

    for l in range(depth):
        lambda_init = 0.8 - 0.6 * math.exp(-0.3 * l)
        proj = _inproj(xf, g_attn3, w_proj, gcol, cos_t, sin_t, bd, layer=l, seq=s, tm=512)
        proj3 = proj.reshape(b, s, PROJ_COLS)
        ya = _swa(proj3, sink, layer=l, tq=512)
        x3 = _diff_outproj(proj3, xf.reshape(b, s, d), ya, lq1_3, lk1_3, lq2_3, lk2_3, subln3, wo,
                           layer=l, lambda_init=lambda_init, tq=512, rows=256)
        xf = _ffn(x3.reshape(b * s, d), g_ffn3, wu, conv_w, conv_b3, wd, layer=l, seq=s, ts=1024)
    return xf.reshape(b, s, d)
```

```python
import functools
import math

import jax
import jax.numpy as jnp
import numpy as np
from jax import lax
from jax.experimental import pallas as pl
from jax.experimental.pallas import tpu as pltpu

D_MODEL = 1024
HEAD_DIM = 64
SWA_Q_HEADS = 8
SWA_KV_HEADS = 2
WINDOW = 128
BLOCK = 128
DIFF_HEADS = 4
DIFF_V_DIM = 128
D_FF = 2816
ROPE_THETA = 10000.0
EPS = 1e-6
NEG = -1e30
LOG2E = 1.4426950408889634

SWA_Q = SWA_Q_HEADS * HEAD_DIM
SWA_KV = SWA_KV_HEADS * HEAD_DIM
DIFF_Q = DIFF_HEADS * 2 * HEAD_DIM
DIFF_V = DIFF_HEADS * DIFF_V_DIM

LANES = 128
MXU_COLS = 256
CHUNK = MXU_COLS
PROJ_COLS = 2560
IN_COLS = SWA_Q + 2 * SWA_KV + 2 * DIFF_Q + DIFF_V
N_IN_CHUNKS = IN_COLS // CHUNK
KV_CHUNK = SWA_Q // CHUNK
NORM_CHUNKS = (0, 1, 2, 3, 4, 5, 6)
FF_CHUNKS = D_FF // CHUNK
HALO = 16

BF16 = jnp.bfloat16
F32 = jnp.float32


def _rms(x):
    return lax.rsqrt(jnp.mean(x * x, axis=-1, keepdims=True) + EPS)


def _layer(shape, l, **kwargs):
    zeros = (0,) * len(shape)
    return pl.BlockSpec((None,) + tuple(shape), lambda *_: (l,) + zeros, **kwargs)


def _inproj_kernel(x_ref, g_ref, w_ref, gcol_ref, cos_ref, sin_ref, bd_ref, o_ref):
    x = x_ref[...]
    h = (x * _rms(x) * g_ref[...]).astype(BF16)
    cos = cos_ref[...]
    sin = sin_ref[...]
    lane = lax.broadcasted_iota(jnp.int32, cos.shape, 1)
    first_half = (lane % HEAD_DIM) < (HEAD_DIM // 2)

    lo = lax.broadcasted_iota(jnp.int32, (x.shape[0], LANES), 1) < HEAD_DIM

    def project(c):
        return jnp.dot(h, w_ref[:, c * CHUNK:(c + 1) * CHUNK], preferred_element_type=F32)

    def twice(pair):
        swapped = pltpu.roll(pair, HEAD_DIM, 1)
        return jnp.concatenate([jnp.where(lo, pair, swapped), jnp.where(lo, swapped, pair)], axis=1)

    p_next = project(0)
    for c in range(N_IN_CHUNKS):
        cols = slice(c * CHUNK, (c + 1) * CHUNK)
        p = p_next
        if c + 1 < N_IN_CHUNKS:
            p_next = project(c + 1)
        y = p
        if c in NORM_CHUNKS:
            ss = jnp.dot((p * p).astype(BF16), bd_ref[...], preferred_element_type=F32)
            y = p * lax.rsqrt(ss * (1.0 / HEAD_DIM) + EPS) * gcol_ref[:, cols]
            partner = jnp.where(first_half,
                                pltpu.roll(y, CHUNK - HEAD_DIM // 2, 1),
                                pltpu.roll(y, HEAD_DIM // 2, 1))
            y = y * cos + partner * sin
        if c == KV_CHUNK:
            o_ref[:, SWA_Q:SWA_Q + CHUNK] = twice(y[:, :LANES]).astype(BF16)
            o_ref[:, SWA_Q + CHUNK:SWA_Q + 2 * CHUNK] = twice(p[:, LANES:]).astype(BF16)
        else:
            out0 = c * CHUNK if c < KV_CHUNK else (c + 1) * CHUNK
            o_ref[:, out0:out0 + CHUNK] = y.astype(BF16)


def _inproj(xf, g, w, gcol, cos_t, sin_t, bd, *, layer, seq, tm):
    t = xf.shape[0]
    tiles_per_seq = seq // tm
    return pl.pallas_call(
        _inproj_kernel,
        grid=(t // tm,),
        in_specs=[
            pl.BlockSpec((tm, D_MODEL), lambda i: (i, 0)),
            _layer((1, D_MODEL), layer),
            _layer((D_MODEL, IN_COLS), layer),
            _layer((1, IN_COLS), layer),
            pl.BlockSpec((tm, CHUNK), lambda i: (i % tiles_per_seq, 0)),
            pl.BlockSpec((tm, CHUNK), lambda i: (i % tiles_per_seq, 0)),
            pl.BlockSpec((CHUNK, CHUNK), lambda i: (0, 0)),
        ],
        out_specs=pl.BlockSpec((tm, PROJ_COLS), lambda i: (i, 0)),
        out_shape=jax.ShapeDtypeStruct((t, PROJ_COLS), BF16),
        compiler_params=pltpu.CompilerParams(
            dimension_semantics=("parallel",), vmem_limit_bytes=48 * 1024 * 1024),
        name="inproj",
    )(xf, g, w, gcol, cos_t, sin_t, bd)


def _swa_kernel(sink_ref, q_ref, kp_ref, kc_ref, kn_ref, vp_ref, vc_ref, vn_ref, o_ref, *,
                layer, ntiles, tq):
    n = pl.program_id(1)
    band = 3 * BLOCK
    blocks = tq // BLOCK
    k_ref_rows = [kp_ref] + [kc_ref] * blocks + [kn_ref]
    v_ref_rows = [vp_ref] + [vc_ref] * blocks + [vn_ref]

    def band_rows(refs, blk, lanes):
        parts = []
        for j in range(blk, blk + 3):
            ref = refs[j]
            parts.append(ref[:, lanes] if j in (0, blocks + 1)
                         else ref[(j - 1) * BLOCK:j * BLOCK, lanes])
        return jnp.concatenate(parts, axis=0)

    row = lax.broadcasted_iota(jnp.int32, (BLOCK, band), 0)
    col = lax.broadcasted_iota(jnp.int32, (BLOCK, band), 1)
    in_window = jnp.abs(col - BLOCK - row) <= WINDOW
    lo = lax.broadcasted_iota(jnp.int32, (BLOCK, LANES), 1) < HEAD_DIM
    hi = jnp.logical_not(lo)
    heads_per_kv = SWA_Q_HEADS // SWA_KV_HEADS
    ones = jnp.ones((band, LANES), BF16)
    zero = jnp.zeros((BLOCK, LANES), BF16)
    units = [(blk, kv) for blk in range(blocks) for kv in range(SWA_KV_HEADS)]

    def scores(unit):
        blk, kv = unit
        stack = []
        for hh in range(heads_per_kv):
            c0 = kv * heads_per_kv * HEAD_DIM + (hh // 2) * LANES
            qh = q_ref[blk * BLOCK:(blk + 1) * BLOCK, c0:c0 + LANES]
            stack.append(jnp.where(lo if hh % 2 == 0 else hi, qh, zero))
        qs = jnp.concatenate(stack, axis=0)
        kk = band_rows(k_ref_rows, blk, slice(kv * LANES, (kv + 1) * LANES))
        return lax.dot_general(qs, kk, (((1,), (1,)), ((), ())), preferred_element_type=F32)

    s_next = scores(units[0])
    for idx, unit in enumerate(units):
        blk, kv = unit
        s = s_next
        if idx + 1 < len(units):
            s_next = scores(units[idx + 1])
        valid = in_window
        if blk == 0:
            valid = valid & ((col >= BLOCK) | (n > 0))
        if blk == blocks - 1:
            valid = valid & ((col < 2 * BLOCK) | (n < ntiles - 1))
        vv = band_rows(v_ref_rows, blk, slice(kv * LANES, (kv + 1) * LANES))
        v1 = jnp.concatenate([vv, ones], axis=1)
        s = jnp.where(valid[None], s.reshape(heads_per_kv, BLOCK, band), NEG)
        s = s.reshape(heads_per_kv * BLOCK, band)
        sk = jnp.concatenate(
            [jnp.full((BLOCK, LANES), sink_ref[layer, kv * heads_per_kv + hh] * LOG2E, F32)
             for hh in range(heads_per_kv)], axis=0)
        m = jnp.maximum(jnp.max(s, axis=-1, keepdims=True), sk)
        p = jnp.concatenate(
            [jnp.exp2(s[:, j * LANES:(j + 1) * LANES] - m) for j in range(band // LANES)],
            axis=1).astype(BF16)
        o2 = jnp.dot(p, v1, preferred_element_type=F32)
        denom = o2[:, LANES:] + jnp.exp2(sk - m)
        o = o2[:, :LANES] / denom
        for pair in range(heads_per_kv // 2):
            out = jnp.where(lo, o[2 * pair * BLOCK:(2 * pair + 1) * BLOCK],
                            o[(2 * pair + 1) * BLOCK:(2 * pair + 2) * BLOCK])
            c0 = kv * heads_per_kv * HEAD_DIM + pair * LANES
            o_ref[blk * BLOCK:(blk + 1) * BLOCK, c0:c0 + LANES] = out.astype(BF16)


def _swa(proj3, sink, *, layer, tq):
    b, s, _ = proj3.shape
    ntiles = s // tq
    per_tile = tq // BLOCK
    last_block = s // BLOCK - 1
    kcol = SWA_Q // CHUNK
    vcol = kcol + 1
    prev = lambda col: pl.BlockSpec(
        (None, BLOCK, CHUNK), lambda bi, n: (bi, jnp.maximum(n * per_tile - 1, 0), col))
    own = lambda col: pl.BlockSpec((None, tq, CHUNK), lambda bi, n: (bi, n, col))
    nxt = lambda col: pl.BlockSpec(
        (None, BLOCK, CHUNK), lambda bi, n: (bi, jnp.minimum((n + 1) * per_tile, last_block), col))
    return pl.pallas_call(
        functools.partial(_swa_kernel, layer=layer, ntiles=ntiles, tq=tq),
        grid=(b, ntiles),
        in_specs=[
            pl.BlockSpec(memory_space=pltpu.SMEM),
            pl.BlockSpec((None, tq, SWA_Q), lambda bi, n: (bi, n, 0)),
            prev(kcol), own(kcol), nxt(kcol),
            prev(vcol), own(vcol), nxt(vcol),
        ],
        out_specs=pl.BlockSpec((None, tq, SWA_Q), lambda bi, n: (bi, n, 0)),
        out_shape=jax.ShapeDtypeStruct((b, s, SWA_Q), BF16),
        compiler_params=pltpu.CompilerParams(dimension_semantics=("parallel", "parallel")),
        name="swa",
    )(sink, proj3, proj3, proj3, proj3, proj3, proj3, proj3)


def _diff_kernel(lq1_ref, lk1_ref, lq2_ref, lk2_ref, subln_ref, q_ref, k_ref, v_ref, x_ref, ya_ref,
                 wa_ref, wb_ref, o_ref, v1_ref, yb_ref, *, lambda_init, tq, rows):
    o_ref[...] = x_ref[...] + jnp.dot(ya_ref[...], wa_ref[...], preferred_element_type=F32)
    lam = (jnp.exp(jnp.sum(lq1_ref[...] * lk1_ref[...], axis=-1, keepdims=True))
           - jnp.exp(jnp.sum(lq2_ref[...] * lk2_ref[...], axis=-1, keepdims=True))
           + lambda_init)

    @pl.when(pl.program_id(1) == 0)
    def _():
        for h in range(DIFF_HEADS):
            v1_ref[h, :, :DIFF_V_DIM] = v_ref[:, h * DIFF_V_DIM:(h + 1) * DIFF_V_DIM]
            v1_ref[h, :, DIFF_V_DIM:] = jnp.ones((v_ref.shape[0], DIFF_V_DIM), BF16)

    lo = lax.broadcasted_iota(jnp.int32, (rows, LANES), 1) < HEAD_DIM
    hi = jnp.logical_not(lo)
    zero = jnp.zeros((rows, LANES), BF16)
    units = [(h, r, c) for h in range(DIFF_HEADS) for r in range(tq // rows) for c in range(2)]

    def scores(unit):
        h, r, c = unit
        qh = q_ref[r * rows:(r + 1) * rows, h * LANES:(h + 1) * LANES]
        qm = jnp.where(lo if c == 0 else hi, qh, zero)
        return lax.dot_general(qm, k_ref[:, h * LANES:(h + 1) * LANES], (((1,), (1,)), ((), ())),
                               preferred_element_type=F32)

    def attend(s, h):
        m = jnp.max(s, axis=-1, keepdims=True)
        p = jnp.exp2(s - m).astype(BF16)
        o2 = jnp.dot(p, v1_ref[h], preferred_element_type=F32)
        return o2[:, :DIFF_V_DIM] / o2[:, DIFF_V_DIM:]

    s_next = scores(units[0])
    first = None
    for idx, unit in enumerate(units):
        h, r, c = unit
        s = s_next
        if idx + 1 < len(units):
            s_next = scores(units[idx + 1])
        o = attend(s, h)
        if c == 0:
            first = o
        else:
            od = first - lam * o
            y = od * _rms(od) * subln_ref[...] * (1.0 - lambda_init)
            yb_ref[r * rows:(r + 1) * rows, h * DIFF_V_DIM:(h + 1) * DIFF_V_DIM] = y.astype(BF16)
    o_ref[...] += jnp.dot(yb_ref[...], wb_ref[...], preferred_element_type=F32)


def _diff_outproj(proj3, x3, ya, lq1, lk1, lq2, lk2, subln, wo, *, layer, lambda_init, tq, rows):
    b, s, _ = proj3.shape
    qcol = (2 * SWA_Q) // DIFF_Q
    return pl.pallas_call(
        functools.partial(_diff_kernel, lambda_init=lambda_init, tq=tq, rows=rows),
        grid=(b, s // tq),
        in_specs=[
            _layer((1, HEAD_DIM), layer), _layer((1, HEAD_DIM), layer),
            _layer((1, HEAD_DIM), layer), _layer((1, HEAD_DIM), layer),
            _layer((1, DIFF_V_DIM), layer),
            pl.BlockSpec((None, tq, DIFF_Q), lambda bi, i: (bi, i, qcol)),
            pl.BlockSpec((None, s, DIFF_Q), lambda bi, i: (bi, 0, qcol + 1)),
            pl.BlockSpec((None, s, DIFF_V), lambda bi, i: (bi, 0, qcol + 2)),
            pl.BlockSpec((None, tq, D_MODEL), lambda bi, i: (bi, i, 0)),
            pl.BlockSpec((None, tq, SWA_Q), lambda bi, i: (bi, i, 0)),
            pl.BlockSpec((None, SWA_Q, D_MODEL), lambda bi, i: (layer, 0, 0)),
            pl.BlockSpec((None, DIFF_V, D_MODEL), lambda bi, i: (layer, 1, 0)),
        ],
        out_specs=pl.BlockSpec((None, tq, D_MODEL), lambda bi, i: (bi, i, 0)),
        out_shape=jax.ShapeDtypeStruct((b, s, D_MODEL), F32),
        scratch_shapes=[pltpu.VMEM((DIFF_HEADS, s, 2 * DIFF_V_DIM), BF16),
                        pltpu.VMEM((tq, DIFF_V), BF16)],
        compiler_params=pltpu.CompilerParams(
            dimension_semantics=("parallel", "arbitrary"),
            vmem_limit_bytes=48 * 1024 * 1024),
        name="diffattn",
    )(lq1, lk1, lq2, lk2, subln, proj3, proj3, proj3, x3, ya, wo, wo)


def _ffn_kernel(xp_ref, xc_ref, xn_ref, g_ref, wu_ref, cw_ref, cb_ref, wd_ref, o_ref,
                hext_ref, *, ts, tiles_per_seq):
    j = pl.program_id(0) % tiles_per_seq
    g = g_ref[...]

    def norm(xx):
        return (xx * _rms(xx) * g).astype(BF16)

    hp = norm(xp_ref[...])
    hn = norm(xn_ref[...])
    hext_ref[0:HALO, :] = jnp.where(j > 0, hp, jnp.zeros_like(hp))
    xc = xc_ref[...]
    hext_ref[HALO:HALO + ts, :] = norm(xc)
    hext_ref[HALO + ts:, :] = jnp.where(j < tiles_per_seq - 1, hn, jnp.zeros_like(hn))

    def up(c):
        gate = jnp.dot(hext_ref[...], wu_ref[:, c * CHUNK:(c + 1) * CHUNK],
                       preferred_element_type=F32)
        val = jnp.dot(hext_ref[HALO:HALO + ts, :], wu_ref[:, D_FF + c * CHUNK:D_FF + (c + 1) * CHUNK],
                      preferred_element_type=F32)
        return gate, val

    ahead = up(0)
    acc = xc
    for c in range(FF_CHUNKS):
        gate, val = ahead
        if c + 1 < FF_CHUNKS:
            ahead = up(c + 1)
        cols = slice(c * CHUNK, (c + 1) * CHUNK)
        cw = cw_ref[:, cols]
        rows = gate.shape[0]
        g_prev = pltpu.roll(gate, 1, 0)[HALO:HALO + ts]
        g_next = pltpu.roll(gate, rows - 1, 0)[HALO:HALO + ts]
        g_cur = gate[HALO:HALO + ts]
        conv = cb_ref[:, cols] + cw[0:1] * g_prev + cw[1:2] * g_cur + cw[2:3] * g_next
        act = conv * (1.0 / (1.0 + jnp.exp(-conv))) * val
        acc = acc + jnp.dot(act.astype(BF16), wd_ref[cols, :], preferred_element_type=F32)
    o_ref[...] = acc


def _ffn(xf, g, wu, cw, cb, wd, *, layer, seq, ts):
    t = xf.shape[0]
    tiles_per_seq = seq // ts
    halo_blocks = ts // HALO
    last_halo = t // HALO - 1
    resident = pl.Buffered(1)
    return pl.pallas_call(
        functools.partial(_ffn_kernel, ts=ts, tiles_per_seq=tiles_per_seq),
        grid=(t // ts,),
        in_specs=[
            pl.BlockSpec((HALO, D_MODEL), lambda i: (jnp.maximum(i * halo_blocks - 1, 0), 0)),
            pl.BlockSpec((ts, D_MODEL), lambda i: (i, 0)),
            pl.BlockSpec((HALO, D_MODEL), lambda i: (jnp.minimum((i + 1) * halo_blocks, last_halo), 0)),
            _layer((1, D_MODEL), layer),
            _layer((D_MODEL, 2 * D_FF), layer, pipeline_mode=resident),
            _layer((3, D_FF), layer),
            _layer((1, D_FF), layer),
            _layer((D_FF, D_MODEL), layer, pipeline_mode=resident),
        ],
        out_specs=pl.BlockSpec((ts, D_MODEL), lambda i: (i, 0)),
        out_shape=jax.ShapeDtypeStruct((t, D_MODEL), F32),
        scratch_shapes=[pltpu.VMEM((ts + 2 * HALO, D_MODEL), BF16)],
        compiler_params=pltpu.CompilerParams(
            dimension_semantics=("parallel",), vmem_limit_bytes=56 * 1024 * 1024),
        name="convglu",
    )(xf, xf, xf, g, wu, cw, cb, wd)


def _rope_tables(seq):
    inv = 1.0 / (ROPE_THETA ** (np.arange(0, HEAD_DIM, 2, dtype=np.float64) / HEAD_DIM))
    ang = np.arange(seq, dtype=np.float64)[:, None] * inv[None, :]
    cos, sin = np.cos(ang), np.sin(ang)
    reps = CHUNK // HEAD_DIM
    cos_t = np.tile(np.concatenate([cos, cos], axis=-1), (1, reps))
    sin_t = np.tile(np.concatenate([-sin, sin], axis=-1), (1, reps))
    return jnp.asarray(cos_t, F32), jnp.asarray(sin_t, F32)


def _head_block_diag():
    idx = np.arange(CHUNK) // HEAD_DIM
    return jnp.asarray(idx[:, None] == idx[None, :], BF16)


def kernel(x, g_attn, w_in, qn_a, kn_a, sink, qn_b, kn_b, lq1, lk1, lq2, lk2, subln, w_out, g_ffn,
           w_up, conv_w, conv_b, w_down):
    b, s, d = x.shape
    depth = w_in.shape[0]
    cos_t, sin_t = _rope_tables(s)
    bd = _head_block_diag()
    scale = HEAD_DIM ** -0.5 * LOG2E
    xf = x.reshape(b * s, d)

    row = lambda p: p[:, None, :]
    heads = lambda g, n: jnp.tile(g, (1, n))
    w_proj = w_in.astype(BF16)
    gcol = row(jnp.concatenate([
        heads(qn_a, SWA_Q_HEADS) * scale, heads(kn_a, SWA_KV_HEADS),
        jnp.ones((depth, SWA_KV), F32),
        heads(qn_b, 2 * DIFF_HEADS) * scale, heads(kn_b, 2 * DIFF_HEADS),
        jnp.ones((depth, DIFF_V), F32)], axis=-1))
    wo = w_out.astype(BF16)
    wu = w_up.astype(BF16)
    wd = w_down.astype(BF16)
    g_attn3, g_ffn3, subln3, conv_b3 = row(g_attn), row(g_ffn), row(subln), row(conv_b)
    lq1_3, lk1_3, lq2_3, lk2_3 = row(lq1), row(lk1), row(lq2), row(lk2)

    for l in range(depth):
        lambda_init = 0.8 - 0.6 * math.exp(-0.3 * l)
        proj = _inproj(xf, g_attn3, w_proj, gcol, cos_t, sin_t, bd, layer=l, seq=s, tm=1024)
        proj3 = proj.reshape(b, s, PROJ_COLS)
        ya = _swa(proj3, sink, layer=l, tq=512)
        x3 = _diff_outproj(proj3, xf.reshape(b, s, d), ya, lq1_3, lk1_3, lq2_3, lk2_3, subln3, wo,
                           layer=l, lambda_init=lambda_init, tq=1024, rows=256)
        xf = _ffn(x3.reshape(b * s, d), g_ffn3, wu, conv_w, conv_b3, wd, layer=l, seq=s, ts=1024)
    return xf.reshape(b, s, d)
```

```python
import functools
import math

import jax
import jax.numpy as jnp
import numpy as np
from jax import lax
from jax.experimental import pallas as pl
from jax.experimental.pallas import tpu as pltpu

D_MODEL = 1024
HEAD_DIM = 64
SWA_Q_HEADS = 8
SWA_KV_HEADS = 2
WINDOW = 128
BLOCK = 128
DIFF_HEADS = 4
DIFF_V_DIM = 128
D_FF = 2816
ROPE_THETA = 10000.0
EPS = 1e-6
NEG = -1e30
LOG2E = 1.4426950408889634

SWA_Q = SWA_Q_HEADS * HEAD_DIM
SWA_KV = SWA_KV_HEADS * HEAD_DIM
DIFF_Q = DIFF_HEADS * 2 * HEAD_DIM
DIFF_V = DIFF_HEADS * DIFF_V_DIM

LANES = 128
MXU_COLS = 256
CHUNK = MXU_COLS
PROJ_COLS = 2560
IN_COLS = SWA_Q + 2 * SWA_KV + 2 * DIFF_Q + DIFF_V
N_IN_CHUNKS = IN_COLS // CHUNK
KV_CHUNK = SWA_Q // CHUNK
NORM_CHUNKS = (0, 1, 2, 3, 4, 5, 6)
FF_CHUNKS = D_FF // CHUNK
HALO = 16

BF16 = jnp.bfloat16
F32 = jnp.float32


def _rms(x):
    return lax.rsqrt(jnp.mean(x * x, axis=-1, keepdims=True) + EPS)


def _layer(shape, l, **kwargs):
    zeros = (0,) * len(shape)
    return pl.BlockSpec((None,) + tuple(shape), lambda *_: (l,) + zeros, **kwargs)


def _inproj_kernel(x_ref, g_ref, w_ref, gcol_ref, cos_ref, sin_ref, bd_ref, o_ref):
    x = x_ref[...]
    h = (x * _rms(x) * g_ref[...]).astype(BF16)
    cos = cos_ref[...]
    sin = sin_ref[...]
    lane = lax.broadcasted_iota(jnp.int32, cos.shape, 1)
    first_half = (lane % HEAD_DIM) < (HEAD_DIM // 2)

    lo = lax.broadcasted_iota(jnp.int32, (x.shape[0], LANES), 1) < HEAD_DIM

    def project(c):
        return jnp.dot(h, w_ref[:, c * CHUNK:(c + 1) * CHUNK], preferred_element_type=F32)

    def twice(pair):
        swapped = pltpu.roll(pair, HEAD_DIM, 1)
        return jnp.concatenate([jnp.where(lo, pair, swapped), jnp.where(lo, swapped, pair)], axis=1)

    p_next = project(0)
    for c in range(N_IN_CHUNKS):
        cols = slice(c * CHUNK, (c + 1) * CHUNK)
        p = p_next
        if c + 1 < N_IN_CHUNKS:
            p_next = project(c + 1)
        y = p
        if c in NORM_CHUNKS:
            ss = jnp.dot((p * p).astype(BF16), bd_ref[...], preferred_element_type=F32)
            y = p * lax.rsqrt(ss * (1.0 / HEAD_DIM) + EPS) * gcol_ref[:, cols]
            partner = jnp.where(first_half,
                                pltpu.roll(y, CHUNK - HEAD_DIM // 2, 1),
                                pltpu.roll(y, HEAD_DIM // 2, 1))
            y = y * cos + partner * sin
        if c == KV_CHUNK:
            o_ref[:, SWA_Q:SWA_Q + CHUNK] = twice(y[:, :LANES]).astype(BF16)
            o_ref[:, SWA_Q + CHUNK:SWA_Q + 2 * CHUNK] = twice(p[:, LANES:]).astype(BF16)
        else:
            out0 = c * CHUNK if c < KV_CHUNK else (c + 1) * CHUNK
            o_ref[:, out0:out0 + CHUNK] = y.astype(BF16)


def _inproj(xf, g, w, gcol, cos_t, sin_t, bd, *, layer, seq, tm):
    t = xf.shape[0]
    tiles_per_seq = seq // tm
    return pl.pallas_call(
        _inproj_kernel,
        grid=(t // tm,),
        in_specs=[
            pl.BlockSpec((tm, D_MODEL), lambda i: (i, 0)),
            _layer((1, D_MODEL), layer),
            _layer((D_MODEL, IN_COLS), layer, pipeline_mode=pl.Buffered(1)),
            _layer((1, IN_COLS), layer),
            pl.BlockSpec((tm, CHUNK), lambda i: (i % tiles_per_seq, 0)),
            pl.BlockSpec((tm, CHUNK), lambda i: (i % tiles_per_seq, 0)),
            pl.BlockSpec((CHUNK, CHUNK), lambda i: (0, 0)),
        ],
        out_specs=pl.BlockSpec((tm, PROJ_COLS), lambda i: (i, 0)),
        out_shape=jax.ShapeDtypeStruct((t, PROJ_COLS), BF16),
        compiler_params=pltpu.CompilerParams(
            dimension_semantics=("parallel",), vmem_limit_bytes=48 * 1024 * 1024),
        name="inproj",
    )(xf, g, w, gcol, cos_t, sin_t, bd)


def _swa_kernel(sink_ref, q_ref, kp_ref, kc_ref, kn_ref, vp_ref, vc_ref, vn_ref, o_ref, *,
                layer, ntiles, tq):
    n = pl.program_id(1)
    band = 3 * BLOCK
    blocks = tq // BLOCK
    k_ref_rows = [kp_ref] + [kc_ref] * blocks + [kn_ref]
    v_ref_rows = [vp_ref] + [vc_ref] * blocks + [vn_ref]

    def band_rows(refs, blk, lanes):
        parts = []
        for j in range(blk, blk + 3):
            ref = refs[j]
            parts.append(ref[:, lanes] if j in (0, blocks + 1)
                         else ref[(j - 1) * BLOCK:j * BLOCK, lanes])
        return jnp.concatenate(parts, axis=0)

    row = lax.broadcasted_iota(jnp.int32, (BLOCK, band), 0)
    col = lax.broadcasted_iota(jnp.int32, (BLOCK, band), 1)
    in_window = jnp.abs(col - BLOCK - row) <= WINDOW
    lo = lax.broadcasted_iota(jnp.int32, (BLOCK, LANES), 1) < HEAD_DIM
    hi = jnp.logical_not(lo)
    heads_per_kv = SWA_Q_HEADS // SWA_KV_HEADS
    ones = jnp.ones((band, LANES), BF16)
    zero = jnp.zeros((BLOCK, LANES), BF16)
    units = [(blk, kv) for blk in range(blocks) for kv in range(SWA_KV_HEADS)]

    def scores(unit):
        blk, kv = unit
        stack = []
        for hh in range(heads_per_kv):
            c0 = kv * heads_per_kv * HEAD_DIM + (hh // 2) * LANES
            qh = q_ref[blk * BLOCK:(blk + 1) * BLOCK, c0:c0 + LANES]
            stack.append(jnp.where(lo if hh % 2 == 0 else hi, qh, zero))
        qs = jnp.concatenate(stack, axis=0)
        kk = band_rows(k_ref_rows, blk, slice(kv * LANES, (kv + 1) * LANES))
        return lax.dot_general(qs, kk, (((1,), (1,)), ((), ())), preferred_element_type=F32)

    s_next = scores(units[0])
    for idx, unit in enumerate(units):
        blk, kv = unit
        s = s_next
        if idx + 1 < len(units):
            s_next = scores(units[idx + 1])
        valid = in_window
        if blk == 0:
            valid = valid & ((col >= BLOCK) | (n > 0))
        if blk == blocks - 1:
            valid = valid & ((col < 2 * BLOCK) | (n < ntiles - 1))
        vv = band_rows(v_ref_rows, blk, slice(kv * LANES, (kv + 1) * LANES))
        v1 = jnp.concatenate([vv, ones], axis=1)
        s = jnp.where(valid[None], s.reshape(heads_per_kv, BLOCK, band), NEG)
        s = s.reshape(heads_per_kv * BLOCK, band)
        sk = jnp.concatenate(
            [jnp.full((BLOCK, LANES), sink_ref[layer, kv * heads_per_kv + hh] * LOG2E, F32)
             for hh in range(heads_per_kv)], axis=0)
        m = jnp.maximum(jnp.max(s, axis=-1, keepdims=True), sk)
        p = jnp.concatenate(
            [jnp.exp2(s[:, j * LANES:(j + 1) * LANES] - m) for j in range(band // LANES)],
            axis=1).astype(BF16)
        o2 = jnp.dot(p, v1, preferred_element_type=F32)
        denom = o2[:, LANES:] + jnp.exp2(sk - m)
        o = o2[:, :LANES] / denom
        for pair in range(heads_per_kv // 2):
            out = jnp.where(lo, o[2 * pair * BLOCK:(2 * pair + 1) * BLOCK],
                            o[(2 * pair + 1) * BLOCK:(2 * pair + 2) * BLOCK])
            c0 = kv * heads_per_kv * HEAD_DIM + pair * LANES
            o_ref[blk * BLOCK:(blk + 1) * BLOCK, c0:c0 + LANES] = out.astype(BF16)


def _swa(proj3, sink, *, layer, tq):
    b, s, _ = proj3.shape
    ntiles = s // tq
    per_tile = tq // BLOCK
    last_block = s // BLOCK - 1
    kcol = SWA_Q // CHUNK
    vcol = kcol + 1
    prev = lambda col: pl.BlockSpec(
        (None, BLOCK, CHUNK), lambda bi, n: (bi, jnp.maximum(n * per_tile - 1, 0), col))
    own = lambda col: pl.BlockSpec((None, tq, CHUNK), lambda bi, n: (bi, n, col))
    nxt = lambda col: pl.BlockSpec(
        (None, BLOCK, CHUNK), lambda bi, n: (bi, jnp.minimum((n + 1) * per_tile, last_block), col))
    return pl.pallas_call(
        functools.partial(_swa_kernel, layer=layer, ntiles=ntiles, tq=tq),
        grid=(b, ntiles),
        in_specs=[
            pl.BlockSpec(memory_space=pltpu.SMEM),
            pl.BlockSpec((None, tq, SWA_Q), lambda bi, n: (bi, n, 0)),
            prev(kcol), own(kcol), nxt(kcol),
            prev(vcol), own(vcol), nxt(vcol),
        ],
        out_specs=pl.BlockSpec((None, tq, SWA_Q), lambda bi, n: (bi, n, 0)),
        out_shape=jax.ShapeDtypeStruct((b, s, SWA_Q), BF16),
        compiler_params=pltpu.CompilerParams(dimension_semantics=("parallel", "parallel")),
        name="swa",
    )(sink, proj3, proj3, proj3, proj3, proj3, proj3, proj3)


def _diff_kernel(lq1_ref, lk1_ref, lq2_ref, lk2_ref, subln_ref, q_ref, k_ref, v_ref, x_ref, ya_ref,
                 wa_ref, wb_ref, o_ref, v1_ref, yb_ref, *, lambda_init, tq, rows):
    o_ref[...] = x_ref[...] + jnp.dot(ya_ref[...], wa_ref[...], preferred_element_type=F32)
    lam = (jnp.exp(jnp.sum(lq1_ref[...] * lk1_ref[...], axis=-1, keepdims=True))
           - jnp.exp(jnp.sum(lq2_ref[...] * lk2_ref[...], axis=-1, keepdims=True))
           + lambda_init)

    @pl.when(pl.program_id(1) == 0)
    def _():
        for h in range(DIFF_HEADS):
            v1_ref[h, :, :DIFF_V_DIM] = v_ref[:, h * DIFF_V_DIM:(h + 1) * DIFF_V_DIM]
            v1_ref[h, :, DIFF_V_DIM:] = jnp.ones((v_ref.shape[0], DIFF_V_DIM), BF16)

    lo = lax.broadcasted_iota(jnp.int32, (rows, LANES), 1) < HEAD_DIM
    hi = jnp.logical_not(lo)
    zero = jnp.zeros((rows, LANES), BF16)
    units = [(h, r, c) for h in range(DIFF_HEADS) for r in range(tq // rows) for c in range(2)]

    def scores(unit):
        h, r, c = unit
        qh = q_ref[r * rows:(r + 1) * rows, h * LANES:(h + 1) * LANES]
        qm = jnp.where(lo if c == 0 else hi, qh, zero)
        return lax.dot_general(qm, k_ref[:, h * LANES:(h + 1) * LANES], (((1,), (1,)), ((), ())),
                               preferred_element_type=F32)

    def attend(s, h):
        m = jnp.max(s, axis=-1, keepdims=True)
        p = jnp.exp2(s - m).astype(BF16)
        o2 = jnp.dot(p, v1_ref[h], preferred_element_type=F32)
        return o2[:, :DIFF_V_DIM] / o2[:, DIFF_V_DIM:]

    s_next = scores(units[0])
    first = None
    for idx, unit in enumerate(units):
        h, r, c = unit
        s = s_next
        if idx + 1 < len(units):
            s_next = scores(units[idx + 1])
        o = attend(s, h)
        if c == 0:
            first = o
        else:
            od = first - lam * o
            y = od * _rms(od) * subln_ref[...] * (1.0 - lambda_init)
            yb_ref[r * rows:(r + 1) * rows, h * DIFF_V_DIM:(h + 1) * DIFF_V_DIM] = y.astype(BF16)
    o_ref[...] += jnp.dot(yb_ref[...], wb_ref[...], preferred_element_type=F32)


def _diff_outproj(proj3, x3, ya, lq1, lk1, lq2, lk2, subln, wo, *, layer, lambda_init, tq, rows):
    b, s, _ = proj3.shape
    qcol = (2 * SWA_Q) // DIFF_Q
    return pl.pallas_call(
        functools.partial(_diff_kernel, lambda_init=lambda_init, tq=tq, rows=rows),
        grid=(b, s // tq),
        in_specs=[
            _layer((1, HEAD_DIM), layer), _layer((1, HEAD_DIM), layer),
            _layer((1, HEAD_DIM), layer), _layer((1, HEAD_DIM), layer),
            _layer((1, DIFF_V_DIM), layer),
            pl.BlockSpec((None, tq, DIFF_Q), lambda bi, i: (bi, i, qcol)),
            pl.BlockSpec((None, s, DIFF_Q), lambda bi, i: (bi, 0, qcol + 1)),
            pl.BlockSpec((None, s, DIFF_V), lambda bi, i: (bi, 0, qcol + 2)),
            pl.BlockSpec((None, tq, D_MODEL), lambda bi, i: (bi, i, 0)),
            pl.BlockSpec((None, tq, SWA_Q), lambda bi, i: (bi, i, 0)),
            pl.BlockSpec((None, SWA_Q, D_MODEL), lambda bi, i: (layer, 0, 0)),
            pl.BlockSpec((None, DIFF_V, D_MODEL), lambda bi, i: (layer, 1, 0)),
        ],
        out_specs=pl.BlockSpec((None, tq, D_MODEL), lambda bi, i: (bi, i, 0)),
        out_shape=jax.ShapeDtypeStruct((b, s, D_MODEL), F32),
        scratch_shapes=[pltpu.VMEM((DIFF_HEADS, s, 2 * DIFF_V_DIM), BF16),
                        pltpu.VMEM((tq, DIFF_V), BF16)],
        compiler_params=pltpu.CompilerParams(
            dimension_semantics=("parallel", "arbitrary"),
            vmem_limit_bytes=48 * 1024 * 1024),
        name="diffattn",
    )(lq1, lk1, lq2, lk2, subln, proj3, proj3, proj3, x3, ya, wo, wo)


def _ffn_kernel(xp_ref, xc_ref, xn_ref, g_ref, wu_ref, cw_ref, cb_ref, wd_ref, o_ref,
                hext_ref, *, ts, tiles_per_seq):
    j = pl.program_id(0) % tiles_per_seq
    g = g_ref[...]

    def norm(xx):
        return (xx * _rms(xx) * g).astype(BF16)

    hp = norm(xp_ref[...])
    hn = norm(xn_ref[...])
    hext_ref[0:HALO, :] = jnp.where(j > 0, hp, jnp.zeros_like(hp))
    xc = xc_ref[...]
    hext_ref[HALO:HALO + ts, :] = norm(xc)
    hext_ref[HALO + ts:, :] = jnp.where(j < tiles_per_seq - 1, hn, jnp.zeros_like(hn))

    def up(c):
        gate = jnp.dot(hext_ref[...], wu_ref[:, c * CHUNK:(c + 1) * CHUNK],
                       preferred_element_type=F32)
        val = jnp.dot(hext_ref[HALO:HALO + ts, :], wu_ref[:, D_FF + c * CHUNK:D_FF + (c + 1) * CHUNK],
                      preferred_element_type=F32)
        return gate, val

    ahead = up(0)
    acc = xc
    for c in range(FF_CHUNKS):
        gate, val = ahead
        if c + 1 < FF_CHUNKS:
            ahead = up(c + 1)
        cols = slice(c * CHUNK, (c + 1) * CHUNK)
        cw = cw_ref[:, cols]
        rows = gate.shape[0]
        g_prev = pltpu.roll(gate, 1, 0)[HALO:HALO + ts]
        g_next = pltpu.roll(gate, rows - 1, 0)[HALO:HALO + ts]
        g_cur = gate[HALO:HALO + ts]
        conv = cb_ref[:, cols] + cw[0:1] * g_prev + cw[1:2] * g_cur + cw[2:3] * g_next
        act = conv * (1.0 / (1.0 + jnp.exp(-conv))) * val
        acc = acc + jnp.dot(act.astype(BF16), wd_ref[cols, :], preferred_element_type=F32)
    o_ref[...] = acc


def _ffn(xf, g, wu, cw, cb, wd, *, layer, seq, ts):
    t = xf.shape[0]
    tiles_per_seq = seq // ts
    halo_blocks = ts // HALO
    last_halo = t // HALO - 1
    resident = pl.Buffered(1)
    return pl.pallas_call(
        functools.partial(_ffn_kernel, ts=ts, tiles_per_seq=tiles_per_seq),
        grid=(t // ts,),
        in_specs=[
            pl.BlockSpec((HALO, D_MODEL), lambda i: (jnp.maximum(i * halo_blocks - 1, 0), 0)),
            pl.BlockSpec((ts, D_MODEL), lambda i: (i, 0)),
            pl.BlockSpec((HALO, D_MODEL), lambda i: (jnp.minimum((i + 1) * halo_blocks, last_halo), 0)),
            _layer((1, D_MODEL), layer),
            _layer((D_MODEL, 2 * D_FF), layer, pipeline_mode=resident),
            _layer((3, D_FF), layer),
            _layer((1, D_FF), layer),
            _layer((D_FF, D_MODEL), layer, pipeline_mode=resident),
        ],
        out_specs=pl.BlockSpec((ts, D_MODEL), lambda i: (i, 0)),
        out_shape=jax.ShapeDtypeStruct((t, D_MODEL), F32),
        scratch_shapes=[pltpu.VMEM((ts + 2 * HALO, D_MODEL), BF16)],
        compiler_params=pltpu.CompilerParams(
            dimension_semantics=("parallel",), vmem_limit_bytes=56 * 1024 * 1024),
        name="convglu",
    )(xf, xf, xf, g, wu, cw, cb, wd)


def _rope_tables(seq):
    inv = 1.0 / (ROPE_THETA ** (np.arange(0, HEAD_DIM, 2, dtype=np.float64) / HEAD_DIM))
    ang = np.arange(seq, dtype=np.float64)[:, None] * inv[None, :]
    cos, sin = np.cos(ang), np.sin(ang)
    reps = CHUNK // HEAD_DIM
    cos_t = np.tile(np.concatenate([cos, cos], axis=-1), (1, reps))
    sin_t = np.tile(np.concatenate([-sin, sin], axis=-1), (1, reps))
    return jnp.asarray(cos_t, F32), jnp.asarray(sin_t, F32)


def _head_block_diag():
    idx = np.arange(CHUNK) // HEAD_DIM
    return jnp.asarray(idx[:, None] == idx[None, :], BF16)


def kernel(x, g_attn, w_in, qn_a, kn_a, sink, qn_b, kn_b, lq1, lk1, lq2, lk2, subln, w_out, g_ffn,
           w_up, conv_w, conv_b, w_down):
    b, s, d = x.shape
    depth = w_in.shape[0]
    cos_t, sin_t = _rope_tables(s)
    bd = _head_block_diag()
    scale = HEAD_DIM ** -0.5 * LOG2E
    xf = x.reshape(b * s, d)

    row = lambda p: p[:, None, :]
    heads = lambda g, n: jnp.tile(g, (1, n))
    w_proj = w_in.astype(BF16)
    gcol = row(jnp.concatenate([
        heads(qn_a, SWA_Q_HEADS) * scale, heads(kn_a, SWA_KV_HEADS),
        jnp.ones((depth, SWA_KV), F32),
        heads(qn_b, 2 * DIFF_HEADS) * scale, heads(kn_b, 2 * DIFF_HEADS),
        jnp.ones((depth, DIFF_V), F32)], axis=-1))
    wo = w_out.astype(BF16)
    wu = w_up.astype(BF16)
    wd = w_down.astype(BF16)
    g_attn3, g_ffn3, subln3, conv_b3 = row(g_attn), row(g_ffn), row(subln), row(conv_b)
    lq1_3, lk1_3, lq2_3, lk2_3 = row(lq1), row(lk1), row(lq2), row(lk2)

    for l in range(depth):
        lambda_init = 0.8 - 0.6 * math.exp(-0.3 * l)
        proj = _inproj(xf, g_attn3, w_proj, gcol, cos_t, sin_t, bd, layer=l, seq=s, tm=1024)
        proj3 = proj.reshape(b, s, PROJ_COLS)
        ya = _swa(proj3, sink, layer=l, tq=1024)
        x3 = _diff_outproj(proj3, xf.reshape(b, s, d), ya, lq1_3, lk1_3, lq2_3, lk2_3, subln3, wo,
                           layer=l, lambda_init=lambda_init, tq=512, rows=256)
        xf = _ffn(x3.reshape(b * s, d), g_ffn3, wu, conv_w, conv_b3, wd, layer=l, seq=s, ts=1024)
    return xf.reshape(b, s, d)
```

```python
import functools
import math

import jax
import jax.numpy as jnp
import numpy as np
from jax import lax
from jax.experimental import pallas as pl
from jax.experimental.pallas import tpu as pltpu

D_MODEL = 1024
HEAD_DIM = 64
SWA_Q_HEADS = 8
SWA_KV_HEADS = 2
WINDOW = 128
BLOCK = 128
DIFF_HEADS = 4
DIFF_V_DIM = 128
D_FF = 2816
ROPE_THETA = 10000.0
EPS = 1e-6
NEG = -1e30
LOG2E = 1.4426950408889634

SWA_Q = SWA_Q_HEADS * HEAD_DIM
SWA_KV = SWA_KV_HEADS * HEAD_DIM
DIFF_Q = DIFF_HEADS * 2 * HEAD_DIM
DIFF_V = DIFF_HEADS * DIFF_V_DIM

LANES = 128
MXU_COLS = 256
CHUNK = MXU_COLS
PROJ_COLS = 2560
IN_COLS = SWA_Q + 2 * SWA_KV + 2 * DIFF_Q + DIFF_V
N_IN_CHUNKS = IN_COLS // CHUNK
KV_CHUNK = SWA_Q // CHUNK
NORM_CHUNKS = (0, 1, 2, 3, 4, 5, 6)
FF_CHUNK = MXU_COLS
HALO = 16

BF16 = jnp.bfloat16
F32 = jnp.float32


def _rms(x):
    return lax.rsqrt(jnp.mean(x * x, axis=-1, keepdims=True) + EPS)


def _layer(shape, l, **kwargs):
    zeros = (0,) * len(shape)
    return pl.BlockSpec((None,) + tuple(shape), lambda *_: (l,) + zeros, **kwargs)


def _inproj_kernel(x_ref, g_ref, w_ref, gcol_ref, cos_ref, sin_ref, bd_ref, o_ref, *, sub):
    tm = x_ref.shape[0]
    lane = lax.broadcasted_iota(jnp.int32, (sub, CHUNK), 1)
    first_half = (lane % HEAD_DIM) < (HEAD_DIM // 2)
    lo = lax.broadcasted_iota(jnp.int32, (sub, LANES), 1) < HEAD_DIM

    def normed(r):
        x = x_ref[r * sub:(r + 1) * sub, :]
        return (x * _rms(x) * g_ref[...]).astype(BF16)

    def twice(pair):
        swapped = pltpu.roll(pair, HEAD_DIM, 1)
        return jnp.concatenate([jnp.where(lo, pair, swapped), jnp.where(lo, swapped, pair)], axis=1)

    items = [(r, c) for r in range(tm // sub) for c in range(N_IN_CHUNKS)]
    hs = {0: normed(0)}

    def project(item):
        r, c = item
        return jnp.dot(hs[r], w_ref[:, c * CHUNK:(c + 1) * CHUNK], preferred_element_type=F32)

    p_next = project(items[0])
    for idx, (r, c) in enumerate(items):
        rows = slice(r * sub, (r + 1) * sub)
        cols = slice(c * CHUNK, (c + 1) * CHUNK)
        p = p_next
        if c == 0 and (r + 1) * sub < tm:
            hs[r + 1] = normed(r + 1)
        if idx + 1 < len(items):
            p_next = project(items[idx + 1])
        y = p
        if c in NORM_CHUNKS:
            ss = jnp.dot((p * p).astype(BF16), bd_ref[...], preferred_element_type=F32)
            y = p * lax.rsqrt(ss * (1.0 / HEAD_DIM) + EPS) * gcol_ref[:, cols]
            partner = jnp.where(first_half,
                                pltpu.roll(y, CHUNK - HEAD_DIM // 2, 1),
                                pltpu.roll(y, HEAD_DIM // 2, 1))
            y = y * cos_ref[rows, :] + partner * sin_ref[rows, :]
        if c == KV_CHUNK:
            o_ref[rows, SWA_Q:SWA_Q + CHUNK] = twice(y[:, :LANES]).astype(BF16)
            o_ref[rows, SWA_Q + CHUNK:SWA_Q + 2 * CHUNK] = twice(p[:, LANES:]).astype(BF16)
        else:
            out0 = c * CHUNK if c < KV_CHUNK else (c + 1) * CHUNK
            o_ref[rows, out0:out0 + CHUNK] = y.astype(BF16)


def _inproj(xf, g, w, gcol, cos_t, sin_t, bd, *, layer, seq, tm, sub):
    t = xf.shape[0]
    tiles_per_seq = seq // tm
    return pl.pallas_call(
        functools.partial(_inproj_kernel, sub=sub),
        grid=(t // tm,),
        in_specs=[
            pl.BlockSpec((tm, D_MODEL), lambda i: (i, 0)),
            _layer((1, D_MODEL), layer),
            _layer((D_MODEL, IN_COLS), layer, pipeline_mode=pl.Buffered(1)),
            _layer((1, IN_COLS), layer),
            pl.BlockSpec((tm, CHUNK), lambda i: (i % tiles_per_seq, 0)),
            pl.BlockSpec((tm, CHUNK), lambda i: (i % tiles_per_seq, 0)),
            pl.BlockSpec((CHUNK, CHUNK), lambda i: (0, 0)),
        ],
        out_specs=pl.BlockSpec((tm, PROJ_COLS), lambda i: (i, 0)),
        out_shape=jax.ShapeDtypeStruct((t, PROJ_COLS), BF16),
        compiler_params=pltpu.CompilerParams(
            dimension_semantics=("parallel",), vmem_limit_bytes=48 * 1024 * 1024),
        name="inproj",
    )(xf, g, w, gcol, cos_t, sin_t, bd)


def _swa_kernel(sink_ref, q_ref, kp_ref, kc_ref, kn_ref, vp_ref, vc_ref, vn_ref, o_ref, *,
                layer, ntiles, tq):
    n = pl.program_id(1)
    band = 3 * BLOCK
    blocks = tq // BLOCK
    k_ref_rows = [kp_ref] + [kc_ref] * blocks + [kn_ref]
    v_ref_rows = [vp_ref] + [vc_ref] * blocks + [vn_ref]

    def band_rows(refs, blk, lanes):
        parts = []
        for j in range(blk, blk + 3):
            ref = refs[j]
            parts.append(ref[:, lanes] if j in (0, blocks + 1)
                         else ref[(j - 1) * BLOCK:j * BLOCK, lanes])
        return jnp.concatenate(parts, axis=0)

    row = lax.broadcasted_iota(jnp.int32, (BLOCK, band), 0)
    col = lax.broadcasted_iota(jnp.int32, (BLOCK, band), 1)
    in_window = jnp.abs(col - BLOCK - row) <= WINDOW
    lo = lax.broadcasted_iota(jnp.int32, (BLOCK, LANES), 1) < HEAD_DIM
    hi = jnp.logical_not(lo)
    heads_per_kv = SWA_Q_HEADS // SWA_KV_HEADS
    ones = jnp.ones((band, LANES), BF16)
    zero = jnp.zeros((BLOCK, LANES), BF16)
    units = [(blk, kv) for blk in range(blocks) for kv in range(SWA_KV_HEADS)]

    def scores(unit):
        blk, kv = unit
        stack = []
        for hh in range(heads_per_kv):
            c0 = kv * heads_per_kv * HEAD_DIM + (hh // 2) * LANES
            qh = q_ref[blk * BLOCK:(blk + 1) * BLOCK, c0:c0 + LANES]
            stack.append(jnp.where(lo if hh % 2 == 0 else hi, qh, zero))
        qs = jnp.concatenate(stack, axis=0)
        kk = band_rows(k_ref_rows, blk, slice(kv * LANES, (kv + 1) * LANES))
        return lax.dot_general(qs, kk, (((1,), (1,)), ((), ())), preferred_element_type=F32)

    s_next = scores(units[0])
    for idx, unit in enumerate(units):
        blk, kv = unit
        s = s_next
        if idx + 1 < len(units):
            s_next = scores(units[idx + 1])
        valid = in_window
        if blk == 0:
            valid = valid & ((col >= BLOCK) | (n > 0))
        if blk == blocks - 1:
            valid = valid & ((col < 2 * BLOCK) | (n < ntiles - 1))
        vv = band_rows(v_ref_rows, blk, slice(kv * LANES, (kv + 1) * LANES))
        v1 = jnp.concatenate([vv, ones], axis=1)
        s = jnp.where(valid[None], s.reshape(heads_per_kv, BLOCK, band), NEG)
        s = s.reshape(heads_per_kv * BLOCK, band)
        sk = jnp.concatenate(
            [jnp.full((BLOCK, LANES), sink_ref[layer, kv * heads_per_kv + hh] * LOG2E, F32)
             for hh in range(heads_per_kv)], axis=0)
        m = jnp.maximum(jnp.max(s, axis=-1, keepdims=True), sk)
        p = jnp.concatenate(
            [jnp.exp2(s[:, j * LANES:(j + 1) * LANES] - m) for j in range(band // LANES)],
            axis=1).astype(BF16)
        o2 = jnp.dot(p, v1, preferred_element_type=F32)
        denom = o2[:, LANES:] + jnp.exp2(sk - m)
        o = o2[:, :LANES] / denom
        for pair in range(heads_per_kv // 2):
            out = jnp.where(lo, o[2 * pair * BLOCK:(2 * pair + 1) * BLOCK],
                            o[(2 * pair + 1) * BLOCK:(2 * pair + 2) * BLOCK])
            c0 = kv * heads_per_kv * HEAD_DIM + pair * LANES
            o_ref[blk * BLOCK:(blk + 1) * BLOCK, c0:c0 + LANES] = out.astype(BF16)


def _swa(proj3, sink, *, layer, tq):
    b, s, _ = proj3.shape
    ntiles = s // tq
    per_tile = tq // BLOCK
    last_block = s // BLOCK - 1
    kcol = SWA_Q // CHUNK
    vcol = kcol + 1
    prev = lambda col: pl.BlockSpec(
        (None, BLOCK, CHUNK), lambda bi, n: (bi, jnp.maximum(n * per_tile - 1, 0), col))
    own = lambda col: pl.BlockSpec((None, tq, CHUNK), lambda bi, n: (bi, n, col))
    nxt = lambda col: pl.BlockSpec(
        (None, BLOCK, CHUNK), lambda bi, n: (bi, jnp.minimum((n + 1) * per_tile, last_block), col))
    return pl.pallas_call(
        functools.partial(_swa_kernel, layer=layer, ntiles=ntiles, tq=tq),
        grid=(b, ntiles),
        in_specs=[
            pl.BlockSpec(memory_space=pltpu.SMEM),
            pl.BlockSpec((None, tq, SWA_Q), lambda bi, n: (bi, n, 0)),
            prev(kcol), own(kcol), nxt(kcol),
            prev(vcol), own(vcol), nxt(vcol),
        ],
        out_specs=pl.BlockSpec((None, tq, SWA_Q), lambda bi, n: (bi, n, 0)),
        out_shape=jax.ShapeDtypeStruct((b, s, SWA_Q), BF16),
        compiler_params=pltpu.CompilerParams(dimension_semantics=("parallel", "parallel")),
        name="swa",
    )(sink, proj3, proj3, proj3, proj3, proj3, proj3, proj3)


def _diff_kernel(lq1_ref, lk1_ref, lq2_ref, lk2_ref, subln_ref, q_ref, k_ref, v_ref, x_ref, ya_ref,
                 wa_ref, wb_ref, o_ref, v1_ref, yb_ref, *, lambda_init, tq, rows):
    o_ref[...] = x_ref[...] + jnp.dot(ya_ref[...], wa_ref[...], preferred_element_type=F32)
    lam = (jnp.exp(jnp.sum(lq1_ref[...] * lk1_ref[...], axis=-1, keepdims=True))
           - jnp.exp(jnp.sum(lq2_ref[...] * lk2_ref[...], axis=-1, keepdims=True))
           + lambda_init)

    @pl.when(pl.program_id(1) == 0)
    def _():
        for h in range(DIFF_HEADS):
            v1_ref[h, :, :DIFF_V_DIM] = v_ref[:, h * DIFF_V_DIM:(h + 1) * DIFF_V_DIM]
            v1_ref[h, :, DIFF_V_DIM:] = jnp.ones((v_ref.shape[0], DIFF_V_DIM), BF16)

    lo = lax.broadcasted_iota(jnp.int32, (rows, LANES), 1) < HEAD_DIM
    hi = jnp.logical_not(lo)
    zero = jnp.zeros((rows, LANES), BF16)
    units = [(h, r, c) for h in range(DIFF_HEADS) for r in range(tq // rows) for c in range(2)]

    def scores(unit):
        h, r, c = unit
        qh = q_ref[r * rows:(r + 1) * rows, h * LANES:(h + 1) * LANES]
        qm = jnp.where(lo if c == 0 else hi, qh, zero)
        return lax.dot_general(qm, k_ref[:, h * LANES:(h + 1) * LANES], (((1,), (1,)), ((), ())),
                               preferred_element_type=F32)

    def attend(s, h):
        m = jnp.max(s, axis=-1, keepdims=True)
        p = jnp.exp2(s - m).astype(BF16)
        o2 = jnp.dot(p, v1_ref[h], preferred_element_type=F32)
        return o2[:, :DIFF_V_DIM] / o2[:, DIFF_V_DIM:]

    s_next = scores(units[0])
    first = None
    for idx, unit in enumerate(units):
        h, r, c = unit
        s = s_next
        if idx + 1 < len(units):
            s_next = scores(units[idx + 1])
        o = attend(s, h)
        if c == 0:
            first = o
        else:
            od = first - lam * o
            y = od * _rms(od) * subln_ref[...] * (1.0 - lambda_init)
            yb_ref[r * rows:(r + 1) * rows, h * DIFF_V_DIM:(h + 1) * DIFF_V_DIM] = y.astype(BF16)
    o_ref[...] += jnp.dot(yb_ref[...], wb_ref[...], preferred_element_type=F32)


def _diff_outproj(proj3, x3, ya, lq1, lk1, lq2, lk2, subln, wo, *, layer, lambda_init, tq, rows):
    b, s, _ = proj3.shape
    qcol = (2 * SWA_Q) // DIFF_Q
    return pl.pallas_call(
        functools.partial(_diff_kernel, lambda_init=lambda_init, tq=tq, rows=rows),
        grid=(b, s // tq),
        in_specs=[
            _layer((1, HEAD_DIM), layer), _layer((1, HEAD_DIM), layer),
            _layer((1, HEAD_DIM), layer), _layer((1, HEAD_DIM), layer),
            _layer((1, DIFF_V_DIM), layer),
            pl.BlockSpec((None, tq, DIFF_Q), lambda bi, i: (bi, i, qcol)),
            pl.BlockSpec((None, s, DIFF_Q), lambda bi, i: (bi, 0, qcol + 1)),
            pl.BlockSpec((None, s, DIFF_V), lambda bi, i: (bi, 0, qcol + 2)),
            pl.BlockSpec((None, tq, D_MODEL), lambda bi, i: (bi, i, 0)),
            pl.BlockSpec((None, tq, SWA_Q), lambda bi, i: (bi, i, 0)),
            pl.BlockSpec((None, SWA_Q, D_MODEL), lambda bi, i: (layer, 0, 0)),
            pl.BlockSpec((None, DIFF_V, D_MODEL), lambda bi, i: (layer, 1, 0)),
        ],
        out_specs=pl.BlockSpec((None, tq, D_MODEL), lambda bi, i: (bi, i, 0)),
        out_shape=jax.ShapeDtypeStruct((b, s, D_MODEL), F32),
        scratch_shapes=[pltpu.VMEM((DIFF_HEADS, s, 2 * DIFF_V_DIM), BF16),
                        pltpu.VMEM((tq, DIFF_V), BF16)],
        compiler_params=pltpu.CompilerParams(
            dimension_semantics=("parallel", "arbitrary"),
            vmem_limit_bytes=48 * 1024 * 1024),
        name="diffattn",
    )(lq1, lk1, lq2, lk2, subln, proj3, proj3, proj3, x3, ya, wo, wo)


def _ffn_kernel(xp_ref, xc_ref, xn_ref, g_ref, wu_ref, cw_ref, cb_ref, wd_ref, o_ref,
                hext_ref, *, ts, sub, tiles_per_seq):
    j = pl.program_id(0) % tiles_per_seq
    g = g_ref[...]
    nsub = ts // sub

    def norm(xx):
        return (xx * _rms(xx) * g).astype(BF16)

    def fill(r):
        lo = 0 if r == 0 else r * sub + 2 * HALO
        hi = (r + 1) * sub + 2 * HALO
        if r == 0:
            hp = norm(xp_ref[...])
            hext_ref[0:HALO, :] = jnp.where(j > 0, hp, jnp.zeros_like(hp))
            lo = HALO
        main_hi = min(hi, HALO + ts)
        hext_ref[lo:main_hi, :] = norm(xc_ref[lo - HALO:main_hi - HALO, :])
        if r == nsub - 1:
            hn = norm(xn_ref[...])
            hext_ref[HALO + ts:, :] = jnp.where(j < tiles_per_seq - 1, hn, jnp.zeros_like(hn))

    chunks = [slice(c0, min(c0 + FF_CHUNK, D_FF)) for c0 in range(0, D_FF, FF_CHUNK)]
    items = [(r, cols) for r in range(nsub) for cols in chunks]

    def up(item):
        r, cols = item
        gate = jnp.dot(hext_ref[r * sub:(r + 1) * sub + 2 * HALO, :], wu_ref[:, cols],
                       preferred_element_type=F32)
        val = jnp.dot(hext_ref[HALO + r * sub:HALO + (r + 1) * sub, :],
                      wu_ref[:, D_FF + cols.start:D_FF + cols.stop], preferred_element_type=F32)
        return gate, val

    fill(0)
    ahead = up(items[0])
    acc = None
    for idx, (r, cols) in enumerate(items):
        gate, val = ahead
        first, last = cols.start == 0, cols.stop == D_FF
        if first and r + 1 < nsub:
            fill(r + 1)
        if idx + 1 < len(items):
            ahead = up(items[idx + 1])
        cw = cw_ref[:, cols]
        rows = gate.shape[0]
        g_prev = pltpu.roll(gate, 1, 0)[HALO:HALO + sub]
        g_next = pltpu.roll(gate, rows - 1, 0)[HALO:HALO + sub]
        g_cur = gate[HALO:HALO + sub]
        conv = cb_ref[:, cols] + cw[0:1] * g_prev + cw[1:2] * g_cur + cw[2:3] * g_next
        act = conv * (1.0 / (1.0 + jnp.exp(-conv))) * val
        if first:
            acc = xc_ref[r * sub:(r + 1) * sub, :]
        acc = acc + jnp.dot(act.astype(BF16), wd_ref[cols, :], preferred_element_type=F32)
        if last:
            o_ref[r * sub:(r + 1) * sub, :] = acc


def _ffn(xf, g, wu, cw, cb, wd, *, layer, seq, ts, sub):
    t = xf.shape[0]
    tiles_per_seq = seq // ts
    halo_blocks = ts // HALO
    last_halo = t // HALO - 1
    resident = pl.Buffered(1)
    return pl.pallas_call(
        functools.partial(_ffn_kernel, ts=ts, sub=sub, tiles_per_seq=tiles_per_seq),
        grid=(t // ts,),
        in_specs=[
            pl.BlockSpec((HALO, D_MODEL), lambda i: (jnp.maximum(i * halo_blocks - 1, 0), 0)),
            pl.BlockSpec((ts, D_MODEL), lambda i: (i, 0)),
            pl.BlockSpec((HALO, D_MODEL), lambda i: (jnp.minimum((i + 1) * halo_blocks, last_halo), 0)),
            _layer((1, D_MODEL), layer),
            _layer((D_MODEL, 2 * D_FF), layer, pipeline_mode=resident),
            _layer((3, D_FF), layer),
            _layer((1, D_FF), layer),
            _layer((D_FF, D_MODEL), layer, pipeline_mode=resident),
        ],
        out_specs=pl.BlockSpec((ts, D_MODEL), lambda i: (i, 0)),
        out_shape=jax.ShapeDtypeStruct((t, D_MODEL), F32),
        scratch_shapes=[pltpu.VMEM((ts + 2 * HALO, D_MODEL), BF16)],
        compiler_params=pltpu.CompilerParams(
            dimension_semantics=("parallel",), vmem_limit_bytes=56 * 1024 * 1024),
        name="convglu",
    )(xf, xf, xf, g, wu, cw, cb, wd)


def _rope_tables(seq):
    inv = 1.0 / (ROPE_THETA ** (np.arange(0, HEAD_DIM, 2, dtype=np.float64) / HEAD_DIM))
    ang = np.arange(seq, dtype=np.float64)[:, None] * inv[None, :]
    cos, sin = np.cos(ang), np.sin(ang)
    reps = CHUNK // HEAD_DIM
    cos_t = np.tile(np.concatenate([cos, cos], axis=-1), (1, reps))
    sin_t = np.tile(np.concatenate([-sin, sin], axis=-1), (1, reps))
    return jnp.asarray(cos_t, F32), jnp.asarray(sin_t, F32)


def _head_block_diag():
    idx = np.arange(CHUNK) // HEAD_DIM
    return jnp.asarray(idx[:, None] == idx[None, :], BF16)


def kernel(x, g_attn, w_in, qn_a, kn_a, sink, qn_b, kn_b, lq1, lk1, lq2, lk2, subln, w_out, g_ffn,
           w_up, conv_w, conv_b, w_down):
    b, s, d = x.shape
    depth = w_in.shape[0]
    cos_t, sin_t = _rope_tables(s)
    bd = _head_block_diag()
    scale = HEAD_DIM ** -0.5 * LOG2E
    xf = x.reshape(b * s, d)

    row = lambda p: p[:, None, :]
    heads = lambda g, n: jnp.tile(g, (1, n))
    w_proj = w_in.astype(BF16)
    gcol = row(jnp.concatenate([
        heads(qn_a, SWA_Q_HEADS) * scale, heads(kn_a, SWA_KV_HEADS),
        jnp.ones((depth, SWA_KV), F32),
        heads(qn_b, 2 * DIFF_HEADS) * scale, heads(kn_b, 2 * DIFF_HEADS),
        jnp.ones((depth, DIFF_V), F32)], axis=-1))
    wo = w_out.astype(BF16)
    wu = w_up.astype(BF16)
    wd = w_down.astype(BF16)
    g_attn3, g_ffn3, subln3, conv_b3 = row(g_attn), row(g_ffn), row(subln), row(conv_b)
    lq1_3, lk1_3, lq2_3, lk2_3 = row(lq1), row(lk1), row(lq2), row(lk2)

    for l in range(depth):
        lambda_init = 0.8 - 0.6 * math.exp(-0.3 * l)
        proj = _inproj(xf, g_attn3, w_proj, gcol, cos_t, sin_t, bd, layer=l, seq=s, tm=1024, sub=512)
        proj3 = proj.reshape(b, s, PROJ_COLS)
        ya = _swa(proj3, sink, layer=l, tq=1024)
        x3 = _diff_outproj(proj3, xf.reshape(b, s, d), ya, lq1_3, lk1_3, lq2_3, lk2_3, subln3, wo,
                           layer=l, lambda_init=lambda_init, tq=512, rows=256)
        xf = _ffn(x3.reshape(b * s, d), g_ffn3, wu, conv_w, conv_b3, wd, layer=l, seq=s, ts=1024, sub=512)
    return xf.reshape(b, s, d)
```

```python
import functools
import math

import jax
import jax.numpy as jnp
import numpy as np
from jax import lax
from jax.experimental import pallas as pl
from jax.experimental.pallas import tpu as pltpu

D_MODEL = 1024
HEAD_DIM = 64
SWA_Q_HEADS = 8
SWA_KV_HEADS = 2
WINDOW = 128
BLOCK = 128
DIFF_HEADS = 4
DIFF_V_DIM = 128
D_FF = 2816
ROPE_THETA = 10000.0
EPS = 1e-6
NEG = -1e30
LOG2E = 1.4426950408889634

SWA_Q = SWA_Q_HEADS * HEAD_DIM
SWA_KV = SWA_KV_HEADS * HEAD_DIM
DIFF_Q = DIFF_HEADS * 2 * HEAD_DIM
DIFF_V = DIFF_HEADS * DIFF_V_DIM

LANES = 128
MXU_COLS = 256
CHUNK = MXU_COLS
PROJ_COLS = 2560
IN_COLS = SWA_Q + 2 * SWA_KV + 2 * DIFF_Q + DIFF_V
N_IN_CHUNKS = IN_COLS // CHUNK
KV_CHUNK = SWA_Q // CHUNK
NORM_CHUNKS = (0, 1, 2, 3, 4, 5, 6)
FF_CHUNK = MXU_COLS
DOWN_GROUP = 4
HALO = 16

BF16 = jnp.bfloat16
F32 = jnp.float32


def _rms(x):
    return lax.rsqrt(jnp.mean(x * x, axis=-1, keepdims=True) + EPS)


def _layer(shape, l, **kwargs):
    zeros = (0,) * len(shape)
    return pl.BlockSpec((None,) + tuple(shape), lambda *_: (l,) + zeros, **kwargs)


def _inproj_kernel(x_ref, g_ref, w_ref, gcol_ref, cos_ref, sin_ref, bd_ref, o_ref, *, sub):
    tm = x_ref.shape[0]
    lane = lax.broadcasted_iota(jnp.int32, (sub, CHUNK), 1)
    first_half = (lane % HEAD_DIM) < (HEAD_DIM // 2)
    lo = lax.broadcasted_iota(jnp.int32, (sub, LANES), 1) < HEAD_DIM

    def normed(r):
        x = x_ref[r * sub:(r + 1) * sub, :]
        return (x * _rms(x) * g_ref[...]).astype(BF16)

    def twice(pair):
        swapped = pltpu.roll(pair, HEAD_DIM, 1)
        return jnp.concatenate([jnp.where(lo, pair, swapped), jnp.where(lo, swapped, pair)], axis=1)

    items = [(r, c) for r in range(tm // sub) for c in range(N_IN_CHUNKS)]
    hs = {0: normed(0)}

    def project(item):
        r, c = item
        return jnp.dot(hs[r], w_ref[:, c * CHUNK:(c + 1) * CHUNK], preferred_element_type=F32)

    p_next = project(items[0])
    for idx, (r, c) in enumerate(items):
        rows = slice(r * sub, (r + 1) * sub)
        cols = slice(c * CHUNK, (c + 1) * CHUNK)
        p = p_next
        if c == 0 and (r + 1) * sub < tm:
            hs[r + 1] = normed(r + 1)
        if idx + 1 < len(items):
            p_next = project(items[idx + 1])
        y = p
        if c in NORM_CHUNKS:
            ss = jnp.dot((p * p).astype(BF16), bd_ref[...], preferred_element_type=F32)
            y = p * lax.rsqrt(ss * (1.0 / HEAD_DIM) + EPS) * gcol_ref[:, cols]
            partner = jnp.where(first_half,
                                pltpu.roll(y, CHUNK - HEAD_DIM // 2, 1),
                                pltpu.roll(y, HEAD_DIM // 2, 1))
            y = y * cos_ref[rows, :] + partner * sin_ref[rows, :]
        if c == KV_CHUNK:
            o_ref[rows, SWA_Q:SWA_Q + CHUNK] = twice(y[:, :LANES]).astype(BF16)
            o_ref[rows, SWA_Q + CHUNK:SWA_Q + 2 * CHUNK] = twice(p[:, LANES:]).astype(BF16)
        else:
            out0 = c * CHUNK if c < KV_CHUNK else (c + 1) * CHUNK
            o_ref[rows, out0:out0 + CHUNK] = y.astype(BF16)


def _inproj(xf, g, w, gcol, cos_t, sin_t, bd, *, layer, seq, tm, sub):
    t = xf.shape[0]
    tiles_per_seq = seq // tm
    return pl.pallas_call(
        functools.partial(_inproj_kernel, sub=sub),
        grid=(t // tm,),
        in_specs=[
            pl.BlockSpec((tm, D_MODEL), lambda i: (i, 0)),
            _layer((1, D_MODEL), layer),
            _layer((D_MODEL, IN_COLS), layer, pipeline_mode=pl.Buffered(1)),
            _layer((1, IN_COLS), layer),
            pl.BlockSpec((tm, CHUNK), lambda i: (i % tiles_per_seq, 0)),
            pl.BlockSpec((tm, CHUNK), lambda i: (i % tiles_per_seq, 0)),
            pl.BlockSpec((CHUNK, CHUNK), lambda i: (0, 0)),
        ],
        out_specs=pl.BlockSpec((tm, PROJ_COLS), lambda i: (i, 0)),
        out_shape=jax.ShapeDtypeStruct((t, PROJ_COLS), BF16),
        compiler_params=pltpu.CompilerParams(
            dimension_semantics=("parallel",), vmem_limit_bytes=48 * 1024 * 1024),
        name="inproj",
    )(xf, g, w, gcol, cos_t, sin_t, bd)


def _swa_kernel(sink_ref, q_ref, kp_ref, kc_ref, kn_ref, vp_ref, vc_ref, vn_ref, o_ref, *,
                layer, ntiles, tq):
    n = pl.program_id(1)
    band = 3 * BLOCK
    blocks = tq // BLOCK
    k_ref_rows = [kp_ref] + [kc_ref] * blocks + [kn_ref]
    v_ref_rows = [vp_ref] + [vc_ref] * blocks + [vn_ref]

    def band_rows(refs, blk, lanes):
        parts = []
        for j in range(blk, blk + 3):
            ref = refs[j]
            parts.append(ref[:, lanes] if j in (0, blocks + 1)
                         else ref[(j - 1) * BLOCK:j * BLOCK, lanes])
        return jnp.concatenate(parts, axis=0)

    row = lax.broadcasted_iota(jnp.int32, (BLOCK, band), 0)
    col = lax.broadcasted_iota(jnp.int32, (BLOCK, band), 1)
    in_window = jnp.abs(col - BLOCK - row) <= WINDOW
    lo = lax.broadcasted_iota(jnp.int32, (BLOCK, LANES), 1) < HEAD_DIM
    hi = jnp.logical_not(lo)
    heads_per_kv = SWA_Q_HEADS // SWA_KV_HEADS
    ones = jnp.ones((band, LANES), BF16)
    zero = jnp.zeros((BLOCK, LANES), BF16)
    units = [(blk, kv) for blk in range(blocks) for kv in range(SWA_KV_HEADS)]

    def scores(unit):
        blk, kv = unit
        stack = []
        for hh in range(heads_per_kv):
            c0 = kv * heads_per_kv * HEAD_DIM + (hh // 2) * LANES
            qh = q_ref[blk * BLOCK:(blk + 1) * BLOCK, c0:c0 + LANES]
            stack.append(jnp.where(lo if hh % 2 == 0 else hi, qh, zero))
        qs = jnp.concatenate(stack, axis=0)
        kk = band_rows(k_ref_rows, blk, slice(kv * LANES, (kv + 1) * LANES))
        return lax.dot_general(qs, kk, (((1,), (1,)), ((), ())), preferred_element_type=F32)

    s_next = scores(units[0])
    for idx, unit in enumerate(units):
        blk, kv = unit
        s = s_next
        if idx + 1 < len(units):
            s_next = scores(units[idx + 1])
        valid = in_window
        if blk == 0:
            valid = valid & ((col >= BLOCK) | (n > 0))
        if blk == blocks - 1:
            valid = valid & ((col < 2 * BLOCK) | (n < ntiles - 1))
        vv = band_rows(v_ref_rows, blk, slice(kv * LANES, (kv + 1) * LANES))
        v1 = jnp.concatenate([vv, ones], axis=1)
        s = jnp.where(valid[None], s.reshape(heads_per_kv, BLOCK, band), NEG)
        s = s.reshape(heads_per_kv * BLOCK, band)
        sk = jnp.concatenate(
            [jnp.full((BLOCK, LANES), sink_ref[layer, kv * heads_per_kv + hh] * LOG2E, F32)
             for hh in range(heads_per_kv)], axis=0)
        m = jnp.maximum(jnp.max(s, axis=-1, keepdims=True), sk)
        p = jnp.concatenate(
            [jnp.exp2(s[:, j * LANES:(j + 1) * LANES] - m) for j in range(band // LANES)],
            axis=1).astype(BF16)
        o2 = jnp.dot(p, v1, preferred_element_type=F32)
        denom = o2[:, LANES:] + jnp.exp2(sk - m)
        o = o2[:, :LANES] / denom
        for pair in range(heads_per_kv // 2):
            out = jnp.where(lo, o[2 * pair * BLOCK:(2 * pair + 1) * BLOCK],
                            o[(2 * pair + 1) * BLOCK:(2 * pair + 2) * BLOCK])
            c0 = kv * heads_per_kv * HEAD_DIM + pair * LANES
            o_ref[blk * BLOCK:(blk + 1) * BLOCK, c0:c0 + LANES] = out.astype(BF16)


def _swa(proj3, sink, *, layer, tq):
    b, s, _ = proj3.shape
    ntiles = s // tq
    per_tile = tq // BLOCK
    last_block = s // BLOCK - 1
    kcol = SWA_Q // CHUNK
    vcol = kcol + 1
    prev = lambda col: pl.BlockSpec(
        (None, BLOCK, CHUNK), lambda bi, n: (bi, jnp.maximum(n * per_tile - 1, 0), col))
    own = lambda col: pl.BlockSpec((None, tq, CHUNK), lambda bi, n: (bi, n, col))
    nxt = lambda col: pl.BlockSpec(
        (None, BLOCK, CHUNK), lambda bi, n: (bi, jnp.minimum((n + 1) * per_tile, last_block), col))
    return pl.pallas_call(
        functools.partial(_swa_kernel, layer=layer, ntiles=ntiles, tq=tq),
        grid=(b, ntiles),
        in_specs=[
            pl.BlockSpec(memory_space=pltpu.SMEM),
            pl.BlockSpec((None, tq, SWA_Q), lambda bi, n: (bi, n, 0)),
            prev(kcol), own(kcol), nxt(kcol),
            prev(vcol), own(vcol), nxt(vcol),
        ],
        out_specs=pl.BlockSpec((None, tq, SWA_Q), lambda bi, n: (bi, n, 0)),
        out_shape=jax.ShapeDtypeStruct((b, s, SWA_Q), BF16),
        compiler_params=pltpu.CompilerParams(dimension_semantics=("parallel", "parallel")),
        name="swa",
    )(sink, proj3, proj3, proj3, proj3, proj3, proj3, proj3)


def _diff_kernel(lq1_ref, lk1_ref, lq2_ref, lk2_ref, subln_ref, q_ref, k_ref, v_ref, x_ref, ya_ref,
                 wa_ref, wb_ref, o_ref, v1_ref, yb_ref, *, lambda_init, tq, rows):
    o_ref[...] = x_ref[...] + jnp.dot(ya_ref[...], wa_ref[...], preferred_element_type=F32)
    lam = (jnp.exp(jnp.sum(lq1_ref[...] * lk1_ref[...], axis=-1, keepdims=True))
           - jnp.exp(jnp.sum(lq2_ref[...] * lk2_ref[...], axis=-1, keepdims=True))
           + lambda_init)

    @pl.when(pl.program_id(1) == 0)
    def _():
        for h in range(DIFF_HEADS):
            v1_ref[h, :, :DIFF_V_DIM] = v_ref[:, h * DIFF_V_DIM:(h + 1) * DIFF_V_DIM]
            v1_ref[h, :, DIFF_V_DIM:] = jnp.ones((v_ref.shape[0], DIFF_V_DIM), BF16)

    lo = lax.broadcasted_iota(jnp.int32, (rows, LANES), 1) < HEAD_DIM
    hi = jnp.logical_not(lo)
    zero = jnp.zeros((rows, LANES), BF16)
    units = [(h, r, c) for h in range(DIFF_HEADS) for r in range(tq // rows) for c in range(2)]

    def scores(unit):
        h, r, c = unit
        qh = q_ref[r * rows:(r + 1) * rows, h * LANES:(h + 1) * LANES]
        qm = jnp.where(lo if c == 0 else hi, qh, zero)
        return lax.dot_general(qm, k_ref[:, h * LANES:(h + 1) * LANES], (((1,), (1,)), ((), ())),
                               preferred_element_type=F32)

    def attend(s, h):
        m = jnp.max(s, axis=-1, keepdims=True)
        p = jnp.exp2(s - m).astype(BF16)
        o2 = jnp.dot(p, v1_ref[h], preferred_element_type=F32)
        return o2[:, :DIFF_V_DIM] / o2[:, DIFF_V_DIM:]

    s_next = scores(units[0])
    first = None
    for idx, unit in enumerate(units):
        h, r, c = unit
        s = s_next
        if idx + 1 < len(units):
            s_next = scores(units[idx + 1])
        o = attend(s, h)
        if c == 0:
            first = o
        else:
            od = first - lam * o
            y = od * _rms(od) * subln_ref[...] * (1.0 - lambda_init)
            yb_ref[r * rows:(r + 1) * rows, h * DIFF_V_DIM:(h + 1) * DIFF_V_DIM] = y.astype(BF16)
    o_ref[...] += jnp.dot(yb_ref[...], wb_ref[...], preferred_element_type=F32)


def _diff_outproj(proj3, x3, ya, lq1, lk1, lq2, lk2, subln, wo, *, layer, lambda_init, tq, rows):
    b, s, _ = proj3.shape
    qcol = (2 * SWA_Q) // DIFF_Q
    return pl.pallas_call(
        functools.partial(_diff_kernel, lambda_init=lambda_init, tq=tq, rows=rows),
        grid=(b, s // tq),
        in_specs=[
            _layer((1, HEAD_DIM), layer), _layer((1, HEAD_DIM), layer),
            _layer((1, HEAD_DIM), layer), _layer((1, HEAD_DIM), layer),
            _layer((1, DIFF_V_DIM), layer),
            pl.BlockSpec((None, tq, DIFF_Q), lambda bi, i: (bi, i, qcol)),
            pl.BlockSpec((None, s, DIFF_Q), lambda bi, i: (bi, 0, qcol + 1)),
            pl.BlockSpec((None, s, DIFF_V), lambda bi, i: (bi, 0, qcol + 2)),
            pl.BlockSpec((None, tq, D_MODEL), lambda bi, i: (bi, i, 0)),
            pl.BlockSpec((None, tq, SWA_Q), lambda bi, i: (bi, i, 0)),
            pl.BlockSpec((None, SWA_Q, D_MODEL), lambda bi, i: (layer, 0, 0)),
            pl.BlockSpec((None, DIFF_V, D_MODEL), lambda bi, i: (layer, 1, 0)),
        ],
        out_specs=pl.BlockSpec((None, tq, D_MODEL), lambda bi, i: (bi, i, 0)),
        out_shape=jax.ShapeDtypeStruct((b, s, D_MODEL), F32),
        scratch_shapes=[pltpu.VMEM((DIFF_HEADS, s, 2 * DIFF_V_DIM), BF16),
                        pltpu.VMEM((tq, DIFF_V), BF16)],
        compiler_params=pltpu.CompilerParams(
            dimension_semantics=("parallel", "arbitrary"),
            vmem_limit_bytes=48 * 1024 * 1024),
        name="diffattn",
    )(lq1, lk1, lq2, lk2, subln, proj3, proj3, proj3, x3, ya, wo, wo)


def _ffn_kernel(xp_ref, xc_ref, xn_ref, g_ref, wu_ref, cw_ref, cb_ref, wd_ref, o_ref,
                hext_ref, *, ts, sub, tiles_per_seq):
    j = pl.program_id(0) % tiles_per_seq
    g = g_ref[...]
    nsub = ts // sub

    def norm(xx):
        return (xx * _rms(xx) * g).astype(BF16)

    def fill(r):
        lo = 0 if r == 0 else r * sub + 2 * HALO
        hi = (r + 1) * sub + 2 * HALO
        if r == 0:
            hp = norm(xp_ref[...])
            hext_ref[0:HALO, :] = jnp.where(j > 0, hp, jnp.zeros_like(hp))
            lo = HALO
        main_hi = min(hi, HALO + ts)
        hext_ref[lo:main_hi, :] = norm(xc_ref[lo - HALO:main_hi - HALO, :])
        if r == nsub - 1:
            hn = norm(xn_ref[...])
            hext_ref[HALO + ts:, :] = jnp.where(j < tiles_per_seq - 1, hn, jnp.zeros_like(hn))

    chunks = [slice(c0, min(c0 + FF_CHUNK, D_FF)) for c0 in range(0, D_FF, FF_CHUNK)]
    items = [(r, cols) for r in range(nsub) for cols in chunks]

    def up(item):
        r, cols = item
        gate = jnp.dot(hext_ref[r * sub:(r + 1) * sub + 2 * HALO, :], wu_ref[:, cols],
                       preferred_element_type=F32)
        val = jnp.dot(hext_ref[HALO + r * sub:HALO + (r + 1) * sub, :],
                      wu_ref[:, D_FF + cols.start:D_FF + cols.stop], preferred_element_type=F32)
        return gate, val

    fill(0)
    ahead = up(items[0])
    acc = None
    for idx, (r, cols) in enumerate(items):
        gate, val = ahead
        first, last = cols.start == 0, cols.stop == D_FF
        if first and r + 1 < nsub:
            fill(r + 1)
        if idx + 1 < len(items):
            ahead = up(items[idx + 1])
        cw = cw_ref[:, cols]
        rows = gate.shape[0]
        g_prev = pltpu.roll(gate, 1, 0)[HALO:HALO + sub]
        g_next = pltpu.roll(gate, rows - 1, 0)[HALO:HALO + sub]
        g_cur = gate[HALO:HALO + sub]
        conv = cb_ref[:, cols] + cw[0:1] * g_prev + cw[1:2] * g_cur + cw[2:3] * g_next
        act = conv * (1.0 / (1.0 + jnp.exp(-conv))) * val
        if first:
            acc = xc_ref[r * sub:(r + 1) * sub, :]
            pending = []
        pending.append((cols, act.astype(BF16)))
        if len(pending) == DOWN_GROUP or last:
            k_rows = slice(pending[0][0].start, pending[-1][0].stop)
            lhs = jnp.concatenate([a for _, a in pending], axis=1) if len(pending) > 1 else pending[0][1]
            acc = acc + jnp.dot(lhs, wd_ref[k_rows, :], preferred_element_type=F32)
            pending = []
        if last:
            o_ref[r * sub:(r + 1) * sub, :] = acc


def _ffn(xf, g, wu, cw, cb, wd, *, layer, seq, ts, sub):
    t = xf.shape[0]
    tiles_per_seq = seq // ts
    halo_blocks = ts // HALO
    last_halo = t // HALO - 1
    resident = pl.Buffered(1)
    return pl.pallas_call(
        functools.partial(_ffn_kernel, ts=ts, sub=sub, tiles_per_seq=tiles_per_seq),
        grid=(t // ts,),
        in_specs=[
            pl.BlockSpec((HALO, D_MODEL), lambda i: (jnp.maximum(i * halo_blocks - 1, 0), 0)),
            pl.BlockSpec((ts, D_MODEL), lambda i: (i, 0)),
            pl.BlockSpec((HALO, D_MODEL), lambda i: (jnp.minimum((i + 1) * halo_blocks, last_halo), 0)),
            _layer((1, D_MODEL), layer),
            _layer((D_MODEL, 2 * D_FF), layer, pipeline_mode=resident),
            _layer((3, D_FF), layer),
            _layer((1, D_FF), layer),
            _layer((D_FF, D_MODEL), layer, pipeline_mode=resident),
        ],
        out_specs=pl.BlockSpec((ts, D_MODEL), lambda i: (i, 0)),
        out_shape=jax.ShapeDtypeStruct((t, D_MODEL), F32),
        scratch_shapes=[pltpu.VMEM((ts + 2 * HALO, D_MODEL), BF16)],
        compiler_params=pltpu.CompilerParams(
            dimension_semantics=("parallel",), vmem_limit_bytes=56 * 1024 * 1024),
        name="convglu",
    )(xf, xf, xf, g, wu, cw, cb, wd)


def _rope_tables(seq):
    inv = 1.0 / (ROPE_THETA ** (np.arange(0, HEAD_DIM, 2, dtype=np.float64) / HEAD_DIM))
    ang = np.arange(seq, dtype=np.float64)[:, None] * inv[None, :]
    cos, sin = np.cos(ang), np.sin(ang)
    reps = CHUNK // HEAD_DIM
    cos_t = np.tile(np.concatenate([cos, cos], axis=-1), (1, reps))
    sin_t = np.tile(np.concatenate([-sin, sin], axis=-1), (1, reps))
    return jnp.asarray(cos_t, F32), jnp.asarray(sin_t, F32)


def _head_block_diag():
    idx = np.arange(CHUNK) // HEAD_DIM
    return jnp.asarray(idx[:, None] == idx[None, :], BF16)


def kernel(x, g_attn, w_in, qn_a, kn_a, sink, qn_b, kn_b, lq1, lk1, lq2, lk2, subln, w_out, g_ffn,
           w_up, conv_w, conv_b, w_down):
    b, s, d = x.shape
    depth = w_in.shape[0]
    cos_t, sin_t = _rope_tables(s)
    bd = _head_block_diag()
    scale = HEAD_DIM ** -0.5 * LOG2E
    xf = x.reshape(b * s, d)

    row = lambda p: p[:, None, :]
    heads = lambda g, n: jnp.tile(g, (1, n))
    w_proj = w_in.astype(BF16)
    gcol = row(jnp.concatenate([
        heads(qn_a, SWA_Q_HEADS) * scale, heads(kn_a, SWA_KV_HEADS),
        jnp.ones((depth, SWA_KV), F32),
        heads(qn_b, 2 * DIFF_HEADS) * scale, heads(kn_b, 2 * DIFF_HEADS),
        jnp.ones((depth, DIFF_V), F32)], axis=-1))
    wo = w_out.astype(BF16)
    wu = w_up.astype(BF16)
    wd = w_down.astype(BF16)
    g_attn3, g_ffn3, subln3, conv_b3 = row(g_attn), row(g_ffn), row(subln), row(conv_b)
    lq1_3, lk1_3, lq2_3, lk2_3 = row(lq1), row(lk1), row(lq2), row(lk2)

    for l in range(depth):
        lambda_init = 0.8 - 0.6 * math.exp(-0.3 * l)
        proj = _inproj(xf, g_attn3, w_proj, gcol, cos_t, sin_t, bd, layer=l, seq=s, tm=1024, sub=512)
        proj3 = proj.reshape(b, s, PROJ_COLS)
        ya = _swa(proj3, sink, layer=l, tq=1024)
        x3 = _diff_outproj(proj3, xf.reshape(b, s, d), ya, lq1_3, lk1_3, lq2_3, lk2_3, subln3, wo,
                           layer=l, lambda_init=lambda_init, tq=512, rows=256)
        xf = _ffn(x3.reshape(b * s, d), g_ffn3, wu, conv_w, conv_b3, wd, layer=l, seq=s, ts=1024, sub=1024)
    return xf.reshape(b, s, d)
```

```python
import functools
import math

import jax
import jax.numpy as jnp
import numpy as np
from jax import lax
from jax.experimental import pallas as pl
from jax.experimental.pallas import tpu as pltpu

D_MODEL = 1024
HEAD_DIM = 64
SWA_Q_HEADS = 8
SWA_KV_HEADS = 2
WINDOW = 128
BLOCK = 128
DIFF_HEADS = 4
DIFF_V_DIM = 128
D_FF = 2816
ROPE_THETA = 10000.0
EPS = 1e-6
NEG = -1e30
LOG2E = 1.4426950408889634

SWA_Q = SWA_Q_HEADS * HEAD_DIM
SWA_KV = SWA_KV_HEADS * HEAD_DIM
DIFF_Q = DIFF_HEADS * 2 * HEAD_DIM
DIFF_V = DIFF_HEADS * DIFF_V_DIM

LANES = 128
MXU_COLS = 256
CHUNK = MXU_COLS
PROJ_COLS = 2560
IN_COLS = SWA_Q + 2 * SWA_KV + 2 * DIFF_Q + DIFF_V
N_IN_CHUNKS = IN_COLS // CHUNK
KV_CHUNK = SWA_Q // CHUNK
NORM_CHUNKS = (0, 1, 2, 3, 4, 5, 6)
FF_CHUNK = MXU_COLS
DOWN_GROUP = 11
HALO = 16

BF16 = jnp.bfloat16
F32 = jnp.float32


def _rms(x):
    return lax.rsqrt(jnp.mean(x * x, axis=-1, keepdims=True) + EPS)


def _layer(shape, l, **kwargs):
    zeros = (0,) * len(shape)
    return pl.BlockSpec((None,) + tuple(shape), lambda *_: (l,) + zeros, **kwargs)


def _inproj_kernel(x_ref, g_ref, w_ref, gcol_ref, cos_ref, sin_ref, bd_ref, o_ref, *, sub):
    tm = x_ref.shape[0]
    lane = lax.broadcasted_iota(jnp.int32, (sub, CHUNK), 1)
    first_half = (lane % HEAD_DIM) < (HEAD_DIM // 2)
    lo = lax.broadcasted_iota(jnp.int32, (sub, LANES), 1) < HEAD_DIM

    def normed(r):
        x = x_ref[r * sub:(r + 1) * sub, :]
        return (x * _rms(x) * g_ref[...]).astype(BF16)

    def twice(pair):
        swapped = pltpu.roll(pair, HEAD_DIM, 1)
        return jnp.concatenate([jnp.where(lo, pair, swapped), jnp.where(lo, swapped, pair)], axis=1)

    items = [(r, c) for r in range(tm // sub) for c in range(N_IN_CHUNKS)]
    hs = {0: normed(0)}

    def project(item):
        r, c = item
        return jnp.dot(hs[r], w_ref[:, c * CHUNK:(c + 1) * CHUNK], preferred_element_type=F32)

    p_next = project(items[0])
    for idx, (r, c) in enumerate(items):
        rows = slice(r * sub, (r + 1) * sub)
        cols = slice(c * CHUNK, (c + 1) * CHUNK)
        p = p_next
        if c == 0 and (r + 1) * sub < tm:
            hs[r + 1] = normed(r + 1)
        if idx + 1 < len(items):
            p_next = project(items[idx + 1])
        y = p
        if c in NORM_CHUNKS:
            ss = jnp.dot((p * p).astype(BF16), bd_ref[...], preferred_element_type=F32)
            y = p * lax.rsqrt(ss * (1.0 / HEAD_DIM) + EPS) * gcol_ref[:, cols]
            partner = jnp.where(first_half,
                                pltpu.roll(y, CHUNK - HEAD_DIM // 2, 1),
                                pltpu.roll(y, HEAD_DIM // 2, 1))
            y = y * cos_ref[rows, :] + partner * sin_ref[rows, :]
        if c == KV_CHUNK:
            o_ref[rows, SWA_Q:SWA_Q + CHUNK] = twice(y[:, :LANES]).astype(BF16)
            o_ref[rows, SWA_Q + CHUNK:SWA_Q + 2 * CHUNK] = twice(p[:, LANES:]).astype(BF16)
        else:
            out0 = c * CHUNK if c < KV_CHUNK else (c + 1) * CHUNK
            o_ref[rows, out0:out0 + CHUNK] = y.astype(BF16)


def _inproj(xf, g, w, gcol, cos_t, sin_t, bd, *, layer, seq, tm, sub):
    t = xf.shape[0]
    tiles_per_seq = seq // tm
    return pl.pallas_call(
        functools.partial(_inproj_kernel, sub=sub),
        grid=(t // tm,),
        in_specs=[
            pl.BlockSpec((tm, D_MODEL), lambda i: (i, 0)),
            _layer((1, D_MODEL), layer),
            _layer((D_MODEL, IN_COLS), layer, pipeline_mode=pl.Buffered(1)),
            _layer((1, IN_COLS), layer),
            pl.BlockSpec((tm, CHUNK), lambda i: (i % tiles_per_seq, 0)),
            pl.BlockSpec((tm, CHUNK), lambda i: (i % tiles_per_seq, 0)),
            pl.BlockSpec((CHUNK, CHUNK), lambda i: (0, 0)),
        ],
        out_specs=pl.BlockSpec((tm, PROJ_COLS), lambda i: (i, 0)),
        out_shape=jax.ShapeDtypeStruct((t, PROJ_COLS), BF16),
        compiler_params=pltpu.CompilerParams(
            dimension_semantics=("parallel",), vmem_limit_bytes=48 * 1024 * 1024),
        name="inproj",
    )(xf, g, w, gcol, cos_t, sin_t, bd)


def _swa_kernel(sink_ref, q_ref, kp_ref, kc_ref, kn_ref, vp_ref, vc_ref, vn_ref, o_ref, *,
                layer, ntiles, tq):
    n = pl.program_id(1)
    band = 3 * BLOCK
    blocks = tq // BLOCK
    k_ref_rows = [kp_ref] + [kc_ref] * blocks + [kn_ref]
    v_ref_rows = [vp_ref] + [vc_ref] * blocks + [vn_ref]

    def band_rows(refs, blk, lanes):
        parts = []
        for j in range(blk, blk + 3):
            ref = refs[j]
            parts.append(ref[:, lanes] if j in (0, blocks + 1)
                         else ref[(j - 1) * BLOCK:j * BLOCK, lanes])
        return jnp.concatenate(parts, axis=0)

    row = lax.broadcasted_iota(jnp.int32, (BLOCK, band), 0)
    col = lax.broadcasted_iota(jnp.int32, (BLOCK, band), 1)
    in_window = jnp.abs(col - BLOCK - row) <= WINDOW
    lo = lax.broadcasted_iota(jnp.int32, (BLOCK, LANES), 1) < HEAD_DIM
    hi = jnp.logical_not(lo)
    heads_per_kv = SWA_Q_HEADS // SWA_KV_HEADS
    ones = jnp.ones((band, LANES), BF16)
    zero = jnp.zeros((BLOCK, LANES), BF16)
    units = [(blk, kv) for blk in range(blocks) for kv in range(SWA_KV_HEADS)]

    def scores(unit):
        blk, kv = unit
        stack = []
        for hh in range(heads_per_kv):
            c0 = kv * heads_per_kv * HEAD_DIM + (hh // 2) * LANES
            qh = q_ref[blk * BLOCK:(blk + 1) * BLOCK, c0:c0 + LANES]
            stack.append(jnp.where(lo if hh % 2 == 0 else hi, qh, zero))
        qs = jnp.concatenate(stack, axis=0)
        kk = band_rows(k_ref_rows, blk, slice(kv * LANES, (kv + 1) * LANES))
        return lax.dot_general(qs, kk, (((1,), (1,)), ((), ())), preferred_element_type=F32)

    s_next = scores(units[0])
    for idx, unit in enumerate(units):
        blk, kv = unit
        s = s_next
        if idx + 1 < len(units):
            s_next = scores(units[idx + 1])
        valid = in_window
        if blk == 0:
            valid = valid & ((col >= BLOCK) | (n > 0))
        if blk == blocks - 1:
            valid = valid & ((col < 2 * BLOCK) | (n < ntiles - 1))
        vv = band_rows(v_ref_rows, blk, slice(kv * LANES, (kv + 1) * LANES))
        v1 = jnp.concatenate([vv, ones], axis=1)
        s = jnp.where(valid[None], s.reshape(heads_per_kv, BLOCK, band), NEG)
        s = s.reshape(heads_per_kv * BLOCK, band)
        sk = jnp.concatenate(
            [jnp.full((BLOCK, LANES), sink_ref[layer, kv * heads_per_kv + hh] * LOG2E, F32)
             for hh in range(heads_per_kv)], axis=0)
        m = jnp.maximum(jnp.max(s, axis=-1, keepdims=True), sk)
        p = jnp.concatenate(
            [jnp.exp2(s[:, j * LANES:(j + 1) * LANES] - m) for j in range(band // LANES)],
            axis=1).astype(BF16)
        o2 = jnp.dot(p, v1, preferred_element_type=F32)
        denom = o2[:, LANES:] + jnp.exp2(sk - m)
        o = o2[:, :LANES] / denom
        for pair in range(heads_per_kv // 2):
            out = jnp.where(lo, o[2 * pair * BLOCK:(2 * pair + 1) * BLOCK],
                            o[(2 * pair + 1) * BLOCK:(2 * pair + 2) * BLOCK])
            c0 = kv * heads_per_kv * HEAD_DIM + pair * LANES
            o_ref[blk * BLOCK:(blk + 1) * BLOCK, c0:c0 + LANES] = out.astype(BF16)


def _swa(proj3, sink, *, layer, tq):
    b, s, _ = proj3.shape
    ntiles = s // tq
    per_tile = tq // BLOCK
    last_block = s // BLOCK - 1
    kcol = SWA_Q // CHUNK
    vcol = kcol + 1
    prev = lambda col: pl.BlockSpec(
        (None, BLOCK, CHUNK), lambda bi, n: (bi, jnp.maximum(n * per_tile - 1, 0), col))
    own = lambda col: pl.BlockSpec((None, tq, CHUNK), lambda bi, n: (bi, n, col))
    nxt = lambda col: pl.BlockSpec(
        (None, BLOCK, CHUNK), lambda bi, n: (bi, jnp.minimum((n + 1) * per_tile, last_block), col))
    return pl.pallas_call(
        functools.partial(_swa_kernel, layer=layer, ntiles=ntiles, tq=tq),
        grid=(b, ntiles),
        in_specs=[
            pl.BlockSpec(memory_space=pltpu.SMEM),
            pl.BlockSpec((None, tq, SWA_Q), lambda bi, n: (bi, n, 0)),
            prev(kcol), own(kcol), nxt(kcol),
            prev(vcol), own(vcol), nxt(vcol),
        ],
        out_specs=pl.BlockSpec((None, tq, SWA_Q), lambda bi, n: (bi, n, 0)),
        out_shape=jax.ShapeDtypeStruct((b, s, SWA_Q), BF16),
        compiler_params=pltpu.CompilerParams(dimension_semantics=("parallel", "parallel")),
        name="swa",
    )(sink, proj3, proj3, proj3, proj3, proj3, proj3, proj3)


def _diff_kernel(lq1_ref, lk1_ref, lq2_ref, lk2_ref, subln_ref, q_ref, k_ref, v_ref, x_ref, ya_ref,
                 wa_ref, wb_ref, o_ref, v1_ref, yb_ref, *, lambda_init, tq, rows):
    o_ref[...] = x_ref[...] + jnp.dot(ya_ref[...], wa_ref[...], preferred_element_type=F32)
    lam = (jnp.exp(jnp.sum(lq1_ref[...] * lk1_ref[...], axis=-1, keepdims=True))
           - jnp.exp(jnp.sum(lq2_ref[...] * lk2_ref[...], axis=-1, keepdims=True))
           + lambda_init)

    @pl.when(pl.program_id(1) == 0)
    def _():
        for h in range(DIFF_HEADS):
            v1_ref[h, :, :DIFF_V_DIM] = v_ref[:, h * DIFF_V_DIM:(h + 1) * DIFF_V_DIM]
            v1_ref[h, :, DIFF_V_DIM:] = jnp.ones((v_ref.shape[0], DIFF_V_DIM), BF16)

    lo = lax.broadcasted_iota(jnp.int32, (rows, LANES), 1) < HEAD_DIM
    hi = jnp.logical_not(lo)
    zero = jnp.zeros((rows, LANES), BF16)
    units = [(h, r, c) for h in range(DIFF_HEADS) for r in range(tq // rows) for c in range(2)]

    def scores(unit):
        h, r, c = unit
        qh = q_ref[r * rows:(r + 1) * rows, h * LANES:(h + 1) * LANES]
        qm = jnp.where(lo if c == 0 else hi, qh, zero)
        return lax.dot_general(qm, k_ref[:, h * LANES:(h + 1) * LANES], (((1,), (1,)), ((), ())),
                               preferred_element_type=F32)

    def attend(s, h):
        m = jnp.max(s, axis=-1, keepdims=True)
        p = jnp.exp2(s - m).astype(BF16)
        o2 = jnp.dot(p, v1_ref[h], preferred_element_type=F32)
        return o2[:, :DIFF_V_DIM] / o2[:, DIFF_V_DIM:]

    s_next = scores(units[0])
    first = None
    for idx, unit in enumerate(units):
        h, r, c = unit
        s = s_next
        if idx + 1 < len(units):
            s_next = scores(units[idx + 1])
        o = attend(s, h)
        if c == 0:
            first = o
        else:
            od = first - lam * o
            y = od * _rms(od) * subln_ref[...] * (1.0 - lambda_init)
            yb_ref[r * rows:(r + 1) * rows, h * DIFF_V_DIM:(h + 1) * DIFF_V_DIM] = y.astype(BF16)
    o_ref[...] += jnp.dot(yb_ref[...], wb_ref[...], preferred_element_type=F32)


def _diff_outproj(proj3, x3, ya, lq1, lk1, lq2, lk2, subln, wo, *, layer, lambda_init, tq, rows):
    b, s, _ = proj3.shape
    qcol = (2 * SWA_Q) // DIFF_Q
    return pl.pallas_call(
        functools.partial(_diff_kernel, lambda_init=lambda_init, tq=tq, rows=rows),
        grid=(b, s // tq),
        in_specs=[
            _layer((1, HEAD_DIM), layer), _layer((1, HEAD_DIM), layer),
            _layer((1, HEAD_DIM), layer), _layer((1, HEAD_DIM), layer),
            _layer((1, DIFF_V_DIM), layer),
            pl.BlockSpec((None, tq, DIFF_Q), lambda bi, i: (bi, i, qcol)),
            pl.BlockSpec((None, s, DIFF_Q), lambda bi, i: (bi, 0, qcol + 1)),
            pl.BlockSpec((None, s, DIFF_V), lambda bi, i: (bi, 0, qcol + 2)),
            pl.BlockSpec((None, tq, D_MODEL), lambda bi, i: (bi, i, 0)),
            pl.BlockSpec((None, tq, SWA_Q), lambda bi, i: (bi, i, 0)),
            pl.BlockSpec((None, SWA_Q, D_MODEL), lambda bi, i: (layer, 0, 0)),
            pl.BlockSpec((None, DIFF_V, D_MODEL), lambda bi, i: (layer, 1, 0)),
        ],
        out_specs=pl.BlockSpec((None, tq, D_MODEL), lambda bi, i: (bi, i, 0)),
        out_shape=jax.ShapeDtypeStruct((b, s, D_MODEL), F32),
        scratch_shapes=[pltpu.VMEM((DIFF_HEADS, s, 2 * DIFF_V_DIM), BF16),
                        pltpu.VMEM((tq, DIFF_V), BF16)],
        compiler_params=pltpu.CompilerParams(
            dimension_semantics=("parallel", "arbitrary"),
            vmem_limit_bytes=48 * 1024 * 1024),
        name="diffattn",
    )(lq1, lk1, lq2, lk2, subln, proj3, proj3, proj3, x3, ya, wo, wo)


def _ffn_kernel(xp_ref, xc_ref, xn_ref, g_ref, wu_ref, cw_ref, cb_ref, wd_ref, o_ref,
                hext_ref, *, ts, sub, tiles_per_seq):
    j = pl.program_id(0) % tiles_per_seq
    g = g_ref[...]
    nsub = ts // sub

    def norm(xx):
        return (xx * _rms(xx) * g).astype(BF16)

    def fill(r):
        lo = 0 if r == 0 else r * sub + 2 * HALO
        hi = (r + 1) * sub + 2 * HALO
        if r == 0:
            hp = norm(xp_ref[...])
            hext_ref[0:HALO, :] = jnp.where(j > 0, hp, jnp.zeros_like(hp))
            lo = HALO
        main_hi = min(hi, HALO + ts)
        hext_ref[lo:main_hi, :] = norm(xc_ref[lo - HALO:main_hi - HALO, :])
        if r == nsub - 1:
            hn = norm(xn_ref[...])
            hext_ref[HALO + ts:, :] = jnp.where(j < tiles_per_seq - 1, hn, jnp.zeros_like(hn))

    chunks = [slice(c0, min(c0 + FF_CHUNK, D_FF)) for c0 in range(0, D_FF, FF_CHUNK)]
    items = [(r, cols) for r in range(nsub) for cols in chunks]

    def up(item):
        r, cols = item
        gate = jnp.dot(hext_ref[r * sub:(r + 1) * sub + 2 * HALO, :], wu_ref[:, cols],
                       preferred_element_type=F32)
        val = jnp.dot(hext_ref[HALO + r * sub:HALO + (r + 1) * sub, :],
                      wu_ref[:, D_FF + cols.start:D_FF + cols.stop], preferred_element_type=F32)
        return gate, val

    fill(0)
    ahead = up(items[0])
    acc = None
    for idx, (r, cols) in enumerate(items):
        gate, val = ahead
        first, last = cols.start == 0, cols.stop == D_FF
        if first and r + 1 < nsub:
            fill(r + 1)
        if idx + 1 < len(items):
            ahead = up(items[idx + 1])
        cw = cw_ref[:, cols]
        rows = gate.shape[0]
        g_prev = pltpu.roll(gate, 1, 0)[HALO:HALO + sub]
        g_next = pltpu.roll(gate, rows - 1, 0)[HALO:HALO + sub]
        g_cur = gate[HALO:HALO + sub]
        conv = cb_ref[:, cols] + cw[0:1] * g_prev + cw[1:2] * g_cur + cw[2:3] * g_next
        act = conv * (1.0 / (1.0 + jnp.exp(-conv))) * val
        if first:
            acc = xc_ref[r * sub:(r + 1) * sub, :]
            pending = []
        pending.append((cols, act.astype(BF16)))
        if len(pending) == DOWN_GROUP or last:
            k_rows = slice(pending[0][0].start, pending[-1][0].stop)
            lhs = jnp.concatenate([a for _, a in pending], axis=1) if len(pending) > 1 else pending[0][1]
            acc = acc + jnp.dot(lhs, wd_ref[k_rows, :], preferred_element_type=F32)
            pending = []
        if last:
            o_ref[r * sub:(r + 1) * sub, :] = acc


def _ffn(xf, g, wu, cw, cb, wd, *, layer, seq, ts, sub):
    t = xf.shape[0]
    tiles_per_seq = seq // ts
    halo_blocks = ts // HALO
    last_halo = t // HALO - 1
    resident = pl.Buffered(1)
    return pl.pallas_call(
        functools.partial(_ffn_kernel, ts=ts, sub=sub, tiles_per_seq=tiles_per_seq),
        grid=(t // ts,),
        in_specs=[
            pl.BlockSpec((HALO, D_MODEL), lambda i: (jnp.maximum(i * halo_blocks - 1, 0), 0)),
            pl.BlockSpec((ts, D_MODEL), lambda i: (i, 0)),
            pl.BlockSpec((HALO, D_MODEL), lambda i: (jnp.minimum((i + 1) * halo_blocks, last_halo), 0)),
            _layer((1, D_MODEL), layer),
            _layer((D_MODEL, 2 * D_FF), layer, pipeline_mode=resident),
            _layer((3, D_FF), layer),
            _layer((1, D_FF), layer),
            _layer((D_FF, D_MODEL), layer, pipeline_mode=resident),
        ],
        out_specs=pl.BlockSpec((ts, D_MODEL), lambda i: (i, 0)),
        out_shape=jax.ShapeDtypeStruct((t, D_MODEL), F32),
        scratch_shapes=[pltpu.VMEM((ts + 2 * HALO, D_MODEL), BF16)],
        compiler_params=pltpu.CompilerParams(
            dimension_semantics=("parallel",), vmem_limit_bytes=56 * 1024 * 1024),
        name="convglu",
    )(xf, xf, xf, g, wu, cw, cb, wd)


def _rope_tables(seq):
    inv = 1.0 / (ROPE_THETA ** (np.arange(0, HEAD_DIM, 2, dtype=np.float64) / HEAD_DIM))
    ang = np.arange(seq, dtype=np.float64)[:, None] * inv[None, :]
    cos, sin = np.cos(ang), np.sin(ang)
    reps = CHUNK // HEAD_DIM
    cos_t = np.tile(np.concatenate([cos, cos], axis=-1), (1, reps))
    sin_t = np.tile(np.concatenate([-sin, sin], axis=-1), (1, reps))
    return jnp.asarray(cos_t, F32), jnp.asarray(sin_t, F32)


def _head_block_diag():
    idx = np.arange(CHUNK) // HEAD_DIM
    return jnp.asarray(idx[:, None] == idx[None, :], BF16)


def kernel(x, g_attn, w_in, qn_a, kn_a, sink, qn_b, kn_b, lq1, lk1, lq2, lk2, subln, w_out, g_ffn,
           w_up, conv_w, conv_b, w_down):
    b, s, d = x.shape
    depth = w_in.shape[0]
    cos_t, sin_t = _rope_tables(s)
    bd = _head_block_diag()
    scale = HEAD_DIM ** -0.5 * LOG2E
    xf = x.reshape(b * s, d)

    row = lambda p: p[:, None, :]
    heads = lambda g, n: jnp.tile(g, (1, n))
    w_proj = w_in.astype(BF16)
    gcol = row(jnp.concatenate([
        heads(qn_a, SWA_Q_HEADS) * scale, heads(kn_a, SWA_KV_HEADS),
        jnp.ones((depth, SWA_KV), F32),
        heads(qn_b, 2 * DIFF_HEADS) * scale, heads(kn_b, 2 * DIFF_HEADS),
        jnp.ones((depth, DIFF_V), F32)], axis=-1))
    wo = w_out.astype(BF16)
    wu = w_up.astype(BF16)
    wd = w_down.astype(BF16)
    g_attn3, g_ffn3, subln3, conv_b3 = row(g_attn), row(g_ffn), row(subln), row(conv_b)
    lq1_3, lk1_3, lq2_3, lk2_3 = row(lq1), row(lk1), row(lq2), row(lk2)

    for l in range(depth):
        lambda_init = 0.8 - 0.6 * math.exp(-0.3 * l)
        proj = _inproj(xf, g_attn3, w_proj, gcol, cos_t, sin_t, bd, layer=l, seq=s, tm=1024, sub=512)
        proj3 = proj.reshape(b, s, PROJ_COLS)
        ya = _swa(proj3, sink, layer=l, tq=1024)
        x3 = _diff_outproj(proj3, xf.reshape(b, s, d), ya, lq1_3, lk1_3, lq2_3, lk2_3, subln3, wo,
                           layer=l, lambda_init=lambda_init, tq=512, rows=256)
        xf = _ffn(x3.reshape(b * s, d), g_ffn3, wu, conv_w, conv_b3, wd, layer=l, seq=s, ts=1024, sub=1024)
    return xf.reshape(b, s, d)
```

```python
import functools
import math

import jax
import jax.numpy as jnp
import numpy as np
from jax import lax
from jax.experimental import pallas as pl
from jax.experimental.pallas import tpu as pltpu

D_MODEL = 1024
HEAD_DIM = 64
SWA_Q_HEADS = 8
SWA_KV_HEADS = 2
WINDOW = 128
BLOCK = 128
DIFF_HEADS = 4
DIFF_V_DIM = 128
D_FF = 2816
ROPE_THETA = 10000.0
EPS = 1e-6
NEG = -1e30
LOG2E = 1.4426950408889634

SWA_Q = SWA_Q_HEADS * HEAD_DIM
SWA_KV = SWA_KV_HEADS * HEAD_DIM
DIFF_Q = DIFF_HEADS * 2 * HEAD_DIM
DIFF_V = DIFF_HEADS * DIFF_V_DIM

LANES = 128
MXU_COLS = 256
CHUNK = MXU_COLS
PROJ_COLS = 2560
IN_COLS = SWA_Q + 2 * SWA_KV + 2 * DIFF_Q + DIFF_V
N_IN_CHUNKS = IN_COLS // CHUNK
KV_CHUNK = SWA_Q // CHUNK
NORM_CHUNKS = (0, 1, 2, 3, 4, 5, 6)
FF_CHUNK = MXU_COLS
DOWN_GROUP = 11
HALO = 16

BF16 = jnp.bfloat16
F32 = jnp.float32


def _rms(x):
    return lax.rsqrt(jnp.mean(x * x, axis=-1, keepdims=True) + EPS)


def _layer(shape, l, **kwargs):
    zeros = (0,) * len(shape)
    return pl.BlockSpec((None,) + tuple(shape), lambda *_: (l,) + zeros, **kwargs)


def _inproj_kernel(x_ref, g_ref, w_ref, gcol_ref, cos_ref, sin_ref, bd_ref, o_ref, *, sub):
    tm = x_ref.shape[0]
    lane = lax.broadcasted_iota(jnp.int32, (sub, CHUNK), 1)
    first_half = (lane % HEAD_DIM) < (HEAD_DIM // 2)
    lo = lax.broadcasted_iota(jnp.int32, (sub, LANES), 1) < HEAD_DIM

    def normed(r):
        x = x_ref[r * sub:(r + 1) * sub, :]
        return (x * _rms(x) * g_ref[...]).astype(BF16)

    def twice(pair):
        swapped = pltpu.roll(pair, HEAD_DIM, 1)
        return jnp.concatenate([jnp.where(lo, pair, swapped), jnp.where(lo, swapped, pair)], axis=1)

    items = [(r, c) for r in range(tm // sub) for c in range(N_IN_CHUNKS)]
    hs = {0: normed(0)}

    def project(item):
        r, c = item
        return jnp.dot(hs[r], w_ref[:, c * CHUNK:(c + 1) * CHUNK], preferred_element_type=F32)

    p_next = project(items[0])
    for idx, (r, c) in enumerate(items):
        rows = slice(r * sub, (r + 1) * sub)
        cols = slice(c * CHUNK, (c + 1) * CHUNK)
        p = p_next
        if c == 0 and (r + 1) * sub < tm:
            hs[r + 1] = normed(r + 1)
        if idx + 1 < len(items):
            p_next = project(items[idx + 1])
        y = p
        if c in NORM_CHUNKS:
            ss = jnp.dot((p * p).astype(BF16), bd_ref[...], preferred_element_type=F32)
            y = p * lax.rsqrt(ss * (1.0 / HEAD_DIM) + EPS) * gcol_ref[:, cols]
            partner = jnp.where(first_half,
                                pltpu.roll(y, CHUNK - HEAD_DIM // 2, 1),
                                pltpu.roll(y, HEAD_DIM // 2, 1))
            y = y * cos_ref[rows, :] + partner * sin_ref[rows, :]
        if c == KV_CHUNK:
            o_ref[rows, SWA_Q:SWA_Q + CHUNK] = twice(y[:, :LANES]).astype(BF16)
            o_ref[rows, SWA_Q + CHUNK:SWA_Q + 2 * CHUNK] = twice(p[:, LANES:]).astype(BF16)
        else:
            out0 = c * CHUNK if c < KV_CHUNK else (c + 1) * CHUNK
            o_ref[rows, out0:out0 + CHUNK] = y.astype(BF16)


def _inproj(xf, g, w, gcol, cos_t, sin_t, bd, *, layer, seq, tm, sub):
    t = xf.shape[0]
    tiles_per_seq = seq // tm
    return pl.pallas_call(
        functools.partial(_inproj_kernel, sub=sub),
        grid=(t // tm,),
        in_specs=[
            pl.BlockSpec((tm, D_MODEL), lambda i: (i, 0)),
            _layer((1, D_MODEL), layer),
            _layer((D_MODEL, IN_COLS), layer, pipeline_mode=pl.Buffered(1)),
            _layer((1, IN_COLS), layer),
            pl.BlockSpec((tm, CHUNK), lambda i: (i % tiles_per_seq, 0)),
            pl.BlockSpec((tm, CHUNK), lambda i: (i % tiles_per_seq, 0)),
            pl.BlockSpec((CHUNK, CHUNK), lambda i: (0, 0)),
        ],
        out_specs=pl.BlockSpec((tm, PROJ_COLS), lambda i: (i, 0)),
        out_shape=jax.ShapeDtypeStruct((t, PROJ_COLS), BF16),
        compiler_params=pltpu.CompilerParams(
            dimension_semantics=("parallel",), vmem_limit_bytes=48 * 1024 * 1024),
        name="inproj",
    )(xf, g, w, gcol, cos_t, sin_t, bd)


def _swa_units(sink_ref, q_ref, k_refs, v_refs, y_ref, *, layer, n, ntiles, tq, after_last):
    band = 3 * BLOCK
    blocks = tq // BLOCK
    kp_ref, kc_ref, kn_ref = k_refs
    vp_ref, vc_ref, vn_ref = v_refs
    k_ref_rows = [kp_ref] + [kc_ref] * blocks + [kn_ref]
    v_ref_rows = [vp_ref] + [vc_ref] * blocks + [vn_ref]

    def band_rows(refs, blk, lanes):
        parts = []
        for j in range(blk, blk + 3):
            ref = refs[j]
            parts.append(ref[:, lanes] if j in (0, blocks + 1)
                         else ref[(j - 1) * BLOCK:j * BLOCK, lanes])
        return jnp.concatenate(parts, axis=0)

    row = lax.broadcasted_iota(jnp.int32, (BLOCK, band), 0)
    col = lax.broadcasted_iota(jnp.int32, (BLOCK, band), 1)
    in_window = jnp.abs(col - BLOCK - row) <= WINDOW
    lo = lax.broadcasted_iota(jnp.int32, (BLOCK, LANES), 1) < HEAD_DIM
    hi = jnp.logical_not(lo)
    heads_per_kv = SWA_Q_HEADS // SWA_KV_HEADS
    ones = jnp.ones((band, LANES), BF16)
    zero = jnp.zeros((BLOCK, LANES), BF16)

    def issue(blk, kv):
        stack = []
        for hh in range(heads_per_kv):
            c0 = kv * heads_per_kv * HEAD_DIM + (hh // 2) * LANES
            qh = q_ref[blk * BLOCK:(blk + 1) * BLOCK, c0:c0 + LANES]
            stack.append(jnp.where(lo if hh % 2 == 0 else hi, qh, zero))
        qs = jnp.concatenate(stack, axis=0)
        kk = band_rows(k_ref_rows, blk, slice(kv * LANES, (kv + 1) * LANES))
        return lax.dot_general(qs, kk, (((1,), (1,)), ((), ())), preferred_element_type=F32)

    def finish(blk, kv, is_last, s):
        valid = in_window
        if blk == 0:
            valid = valid & ((col >= BLOCK) | (n > 0))
        if blk == blocks - 1:
            valid = valid & ((col < 2 * BLOCK) | (n < ntiles - 1))
        vv = band_rows(v_ref_rows, blk, slice(kv * LANES, (kv + 1) * LANES))
        v1 = jnp.concatenate([vv, ones], axis=1)
        s = jnp.where(valid[None], s.reshape(heads_per_kv, BLOCK, band), NEG)
        s = s.reshape(heads_per_kv * BLOCK, band)
        sk = jnp.concatenate(
            [jnp.full((BLOCK, LANES), sink_ref[layer, kv * heads_per_kv + hh] * LOG2E, F32)
             for hh in range(heads_per_kv)], axis=0)
        m = jnp.maximum(jnp.max(s, axis=-1, keepdims=True), sk)
        p = jnp.concatenate(
            [jnp.exp2(s[:, j * LANES:(j + 1) * LANES] - m) for j in range(band // LANES)],
            axis=1).astype(BF16)
        o2 = jnp.dot(p, v1, preferred_element_type=F32)
        denom = o2[:, LANES:] + jnp.exp2(sk - m)
        o = o2[:, :LANES] / denom
        for pair in range(heads_per_kv // 2):
            out = jnp.where(lo, o[2 * pair * BLOCK:(2 * pair + 1) * BLOCK],
                            o[(2 * pair + 1) * BLOCK:(2 * pair + 2) * BLOCK])
            c0 = kv * heads_per_kv * HEAD_DIM + pair * LANES
            y_ref[blk * BLOCK:(blk + 1) * BLOCK, c0:c0 + LANES] = out.astype(BF16)
        if is_last:
            after_last()

    keys = [(blk, kv) for blk in range(blocks) for kv in range(SWA_KV_HEADS)]
    return [(functools.partial(issue, blk, kv),
             functools.partial(finish, blk, kv, (blk, kv) == keys[-1])) for blk, kv in keys]


def _diff_units(lam, subln_ref, q_ref, k_ref, v1_ref, y_ref, *, lambda_init, tq, rows):
    lo = lax.broadcasted_iota(jnp.int32, (rows, LANES), 1) < HEAD_DIM
    hi = jnp.logical_not(lo)
    zero = jnp.zeros((rows, LANES), BF16)
    held = {}

    def issue(h, r, c):
        qh = q_ref[r * rows:(r + 1) * rows, h * LANES:(h + 1) * LANES]
        qm = jnp.where(lo if c == 0 else hi, qh, zero)
        return lax.dot_general(qm, k_ref[:, h * LANES:(h + 1) * LANES], (((1,), (1,)), ((), ())),
                               preferred_element_type=F32)

    def finish(h, r, c, s):
        m = jnp.max(s, axis=-1, keepdims=True)
        p = jnp.exp2(s - m).astype(BF16)
        o2 = jnp.dot(p, v1_ref[h], preferred_element_type=F32)
        o = o2[:, :DIFF_V_DIM] / o2[:, DIFF_V_DIM:]
        if c == 0:
            held[(h, r)] = o
        else:
            od = held.pop((h, r)) - lam * o
            y = od * _rms(od) * subln_ref[...] * (1.0 - lambda_init)
            c0 = SWA_Q + h * DIFF_V_DIM
            y_ref[r * rows:(r + 1) * rows, c0:c0 + DIFF_V_DIM] = y.astype(BF16)

    keys = [(h, r, c) for h in range(DIFF_HEADS) for r in range(tq // rows) for c in range(2)]
    return [(functools.partial(issue, *key), functools.partial(finish, *key)) for key in keys]


def _attn_kernel(sink_ref, lq1_ref, lk1_ref, lq2_ref, lk2_ref, subln_ref,
                 qa_ref, kp_ref, kc_ref, kn_ref, vp_ref, vc_ref, vn_ref, qb_ref, kb_ref, vb_ref,
                 x_ref, wa_ref, wb_ref, o_ref, v1_ref, y_ref, *, layer, lambda_init, ntiles, tq, rows):
    n = pl.program_id(1)
    lam = (jnp.exp(jnp.sum(lq1_ref[...] * lk1_ref[...], axis=-1, keepdims=True))
           - jnp.exp(jnp.sum(lq2_ref[...] * lk2_ref[...], axis=-1, keepdims=True))
           + lambda_init)

    @pl.when(n == 0)
    def _():
        for h in range(DIFF_HEADS):
            v1_ref[h, :, :DIFF_V_DIM] = vb_ref[:, h * DIFF_V_DIM:(h + 1) * DIFF_V_DIM]
            v1_ref[h, :, DIFF_V_DIM:] = jnp.ones((vb_ref.shape[0], DIFF_V_DIM), BF16)

    def project_group_a():
        o_ref[...] = x_ref[...] + jnp.dot(y_ref[:, :SWA_Q], wa_ref[...], preferred_element_type=F32)

    units = _swa_units(sink_ref, qa_ref, (kp_ref, kc_ref, kn_ref), (vp_ref, vc_ref, vn_ref), y_ref,
                       layer=layer, n=n, ntiles=ntiles, tq=tq, after_last=project_group_a)
    units += _diff_units(lam, subln_ref, qb_ref, kb_ref, v1_ref, y_ref,
                         lambda_init=lambda_init, tq=tq, rows=rows)
    s_next = units[0][0]()
    for idx, (_, finish) in enumerate(units):
        s = s_next
        if idx + 1 < len(units):
            s_next = units[idx + 1][0]()
        finish(s)
    o_ref[...] += jnp.dot(y_ref[:, SWA_Q:], wb_ref[...], preferred_element_type=F32)


def _attn(proj3, x3, sink, lq1, lk1, lq2, lk2, subln, wo, *, layer, lambda_init, tq, rows):
    b, s, _ = proj3.shape
    ntiles = s // tq
    per_tile = tq // BLOCK
    last_block = s // BLOCK - 1
    kcol = SWA_Q // CHUNK
    qcol = (2 * SWA_Q) // DIFF_Q
    prev = lambda col: pl.BlockSpec(
        (None, BLOCK, CHUNK), lambda bi, n: (bi, jnp.maximum(n * per_tile - 1, 0), col))
    own = lambda col: pl.BlockSpec((None, tq, CHUNK), lambda bi, n: (bi, n, col))
    nxt = lambda col: pl.BlockSpec(
        (None, BLOCK, CHUNK), lambda bi, n: (bi, jnp.minimum((n + 1) * per_tile, last_block), col))
    return pl.pallas_call(
        functools.partial(_attn_kernel, layer=layer, lambda_init=lambda_init, ntiles=ntiles, tq=tq,
                          rows=rows),
        grid=(b, ntiles),
        in_specs=[
            pl.BlockSpec(memory_space=pltpu.SMEM),
            _layer((1, HEAD_DIM), layer), _layer((1, HEAD_DIM), layer),
            _layer((1, HEAD_DIM), layer), _layer((1, HEAD_DIM), layer),
            _layer((1, DIFF_V_DIM), layer),
            pl.BlockSpec((None, tq, SWA_Q), lambda bi, n: (bi, n, 0)),
            prev(kcol), own(kcol), nxt(kcol),
            prev(kcol + 1), own(kcol + 1), nxt(kcol + 1),
            pl.BlockSpec((None, tq, DIFF_Q), lambda bi, n: (bi, n, qcol)),
            pl.BlockSpec((None, s, DIFF_Q), lambda bi, n: (bi, 0, qcol + 1)),
            pl.BlockSpec((None, s, DIFF_V), lambda bi, n: (bi, 0, qcol + 2)),
            pl.BlockSpec((None, tq, D_MODEL), lambda bi, n: (bi, n, 0)),
            pl.BlockSpec((None, SWA_Q, D_MODEL), lambda bi, n: (layer, 0, 0)),
            pl.BlockSpec((None, DIFF_V, D_MODEL), lambda bi, n: (layer, 1, 0)),
        ],
        out_specs=pl.BlockSpec((None, tq, D_MODEL), lambda bi, n: (bi, n, 0)),
        out_shape=jax.ShapeDtypeStruct((b, s, D_MODEL), F32),
        scratch_shapes=[pltpu.VMEM((DIFF_HEADS, s, 2 * DIFF_V_DIM), BF16),
                        pltpu.VMEM((tq, SWA_Q + DIFF_V), BF16)],
        compiler_params=pltpu.CompilerParams(
            dimension_semantics=("parallel", "arbitrary"),
            vmem_limit_bytes=48 * 1024 * 1024),
        name="attn",
    )(sink, lq1, lk1, lq2, lk2, subln, proj3, proj3, proj3, proj3, proj3, proj3, proj3,
      proj3, proj3, proj3, x3, wo, wo)


def _ffn_kernel(xp_ref, xc_ref, xn_ref, g_ref, wu_ref, cw_ref, cb_ref, wd_ref, o_ref,
                hext_ref, *, ts, sub, tiles_per_seq):
    j = pl.program_id(0) % tiles_per_seq
    g = g_ref[...]
    nsub = ts // sub

    def norm(xx):
        return (xx * _rms(xx) * g).astype(BF16)

    def fill(r):
        lo = 0 if r == 0 else r * sub + 2 * HALO
        hi = (r + 1) * sub + 2 * HALO
        if r == 0:
            hp = norm(xp_ref[...])
            hext_ref[0:HALO, :] = jnp.where(j > 0, hp, jnp.zeros_like(hp))
            lo = HALO
        main_hi = min(hi, HALO + ts)
        hext_ref[lo:main_hi, :] = norm(xc_ref[lo - HALO:main_hi - HALO, :])
        if r == nsub - 1:
            hn = norm(xn_ref[...])
            hext_ref[HALO + ts:, :] = jnp.where(j < tiles_per_seq - 1, hn, jnp.zeros_like(hn))

    chunks = [slice(c0, min(c0 + FF_CHUNK, D_FF)) for c0 in range(0, D_FF, FF_CHUNK)]
    items = [(r, cols) for r in range(nsub) for cols in chunks]

    def up(item):
        r, cols = item
        gate = jnp.dot(hext_ref[r * sub:(r + 1) * sub + 2 * HALO, :], wu_ref[:, cols],
                       preferred_element_type=F32)
        val = jnp.dot(hext_ref[HALO + r * sub:HALO + (r + 1) * sub, :],
                      wu_ref[:, D_FF + cols.start:D_FF + cols.stop], preferred_element_type=F32)
        return gate, val

    fill(0)
    ahead = up(items[0])
    acc = None
    for idx, (r, cols) in enumerate(items):
        gate, val = ahead
        first, last = cols.start == 0, cols.stop == D_FF
        if first and r + 1 < nsub:
            fill(r + 1)
        if idx + 1 < len(items):
            ahead = up(items[idx + 1])
        cw = cw_ref[:, cols]
        rows = gate.shape[0]
        g_prev = pltpu.roll(gate, 1, 0)[HALO:HALO + sub]
        g_next = pltpu.roll(gate, rows - 1, 0)[HALO:HALO + sub]
        g_cur = gate[HALO:HALO + sub]
        conv = cb_ref[:, cols] + cw[0:1] * g_prev + cw[1:2] * g_cur + cw[2:3] * g_next
        act = conv * (1.0 / (1.0 + jnp.exp(-conv))) * val
        if first:
            acc = xc_ref[r * sub:(r + 1) * sub, :]
            pending = []
        pending.append((cols, act.astype(BF16)))
        if len(pending) == DOWN_GROUP or last:
            k_rows = slice(pending[0][0].start, pending[-1][0].stop)
            lhs = jnp.concatenate([a for _, a in pending], axis=1) if len(pending) > 1 else pending[0][1]
            acc = acc + jnp.dot(lhs, wd_ref[k_rows, :], preferred_element_type=F32)
            pending = []
        if last:
            o_ref[r * sub:(r + 1) * sub, :] = acc


def _ffn(xf, g, wu, cw, cb, wd, *, layer, seq, ts, sub):
    t = xf.shape[0]
    tiles_per_seq = seq // ts
    halo_blocks = ts // HALO
    last_halo = t // HALO - 1
    resident = pl.Buffered(1)
    return pl.pallas_call(
        functools.partial(_ffn_kernel, ts=ts, sub=sub, tiles_per_seq=tiles_per_seq),
        grid=(t // ts,),
        in_specs=[
            pl.BlockSpec((HALO, D_MODEL), lambda i: (jnp.maximum(i * halo_blocks - 1, 0), 0)),
            pl.BlockSpec((ts, D_MODEL), lambda i: (i, 0)),
            pl.BlockSpec((HALO, D_MODEL), lambda i: (jnp.minimum((i + 1) * halo_blocks, last_halo), 0)),
            _layer((1, D_MODEL), layer),
            _layer((D_MODEL, 2 * D_FF), layer, pipeline_mode=resident),
            _layer((3, D_FF), layer),
            _layer((1, D_FF), layer),
            _layer((D_FF, D_MODEL), layer, pipeline_mode=resident),
        ],
        out_specs=pl.BlockSpec((ts, D_MODEL), lambda i: (i, 0)),
        out_shape=jax.ShapeDtypeStruct((t, D_MODEL), F32),
        scratch_shapes=[pltpu.VMEM((ts + 2 * HALO, D_MODEL), BF16)],
        compiler_params=pltpu.CompilerParams(
            dimension_semantics=("parallel",), vmem_limit_bytes=56 * 1024 * 1024),
        name="convglu",
    )(xf, xf, xf, g, wu, cw, cb, wd)


def _rope_tables(seq):
    inv = 1.0 / (ROPE_THETA ** (np.arange(0, HEAD_DIM, 2, dtype=np.float64) / HEAD_DIM))
    ang = np.arange(seq, dtype=np.float64)[:, None] * inv[None, :]
    cos, sin = np.cos(ang), np.sin(ang)
    reps = CHUNK // HEAD_DIM
    cos_t = np.tile(np.concatenate([cos, cos], axis=-1), (1, reps))
    sin_t = np.tile(np.concatenate([-sin, sin], axis=-1), (1, reps))
    return jnp.asarray(cos_t, F32), jnp.asarray(sin_t, F32)


def _head_block_diag():
    idx = np.arange(CHUNK) // HEAD_DIM
    return jnp.asarray(idx[:, None] == idx[None, :], BF16)


def kernel(x, g_attn, w_in, qn_a, kn_a, sink, qn_b, kn_b, lq1, lk1, lq2, lk2, subln, w_out, g_ffn,
           w_up, conv_w, conv_b, w_down):
    b, s, d = x.shape
    depth = w_in.shape[0]
    cos_t, sin_t = _rope_tables(s)
    bd = _head_block_diag()
    scale = HEAD_DIM ** -0.5 * LOG2E
    xf = x.reshape(b * s, d)

    row = lambda p: p[:, None, :]
    heads = lambda g, n: jnp.tile(g, (1, n))
    w_proj = w_in.astype(BF16)
    gcol = row(jnp.concatenate([
        heads(qn_a, SWA_Q_HEADS) * scale, heads(kn_a, SWA_KV_HEADS),
        jnp.ones((depth, SWA_KV), F32),
        heads(qn_b, 2 * DIFF_HEADS) * scale, heads(kn_b, 2 * DIFF_HEADS),
        jnp.ones((depth, DIFF_V), F32)], axis=-1))
    wo = w_out.astype(BF16)
    wu = w_up.astype(BF16)
    wd = w_down.astype(BF16)
    g_attn3, g_ffn3, subln3, conv_b3 = row(g_attn), row(g_ffn), row(subln), row(conv_b)
    lq1_3, lk1_3, lq2_3, lk2_3 = row(lq1), row(lk1), row(lq2), row(lk2)

    for l in range(depth):
        lambda_init = 0.8 - 0.6 * math.exp(-0.3 * l)
        proj = _inproj(xf, g_attn3, w_proj, gcol, cos_t, sin_t, bd, layer=l, seq=s, tm=1024, sub=512)
        proj3 = proj.reshape(b, s, PROJ_COLS)
        x3 = _attn(proj3, xf.reshape(b, s, d), sink, lq1_3, lk1_3, lq2_3, lk2_3, subln3, wo,
                   layer=l, lambda_init=lambda_init, tq=512, rows=256)
        xf = _ffn(x3.reshape(b * s, d), g_ffn3, wu, conv_w, conv_b3, wd, layer=l, seq=s, ts=1024, sub=1024)
    return xf.reshape(b, s, d)
```

```python
import functools
import math

import jax
import jax.numpy as jnp
import numpy as np
from jax import lax
from jax.experimental import pallas as pl
from jax.experimental.pallas import tpu as pltpu

D_MODEL = 1024
HEAD_DIM = 64
SWA_Q_HEADS = 8
SWA_KV_HEADS = 2
WINDOW = 128
BLOCK = 128
DIFF_HEADS = 4
DIFF_V_DIM = 128
D_FF = 2816
ROPE_THETA = 10000.0
EPS = 1e-6
NEG = -1e30
LOG2E = 1.4426950408889634

SWA_Q = SWA_Q_HEADS * HEAD_DIM
SWA_KV = SWA_KV_HEADS * HEAD_DIM
DIFF_Q = DIFF_HEADS * 2 * HEAD_DIM
DIFF_V = DIFF_HEADS * DIFF_V_DIM

LANES = 128
MXU_COLS = 256
CHUNK = MXU_COLS
PROJ_COLS = 2560
IN_COLS = SWA_Q + 2 * SWA_KV + 2 * DIFF_Q + DIFF_V
N_IN_CHUNKS = IN_COLS // CHUNK
KV_CHUNK = SWA_Q // CHUNK
NORM_CHUNKS = (0, 1, 2, 3, 4, 5, 6)
FF_CHUNK = MXU_COLS
HALO = 16

MIB = 1024 * 1024
INPROJ_TILE, INPROJ_SUBTILE, INPROJ_VMEM = 1024, 512, 48 * MIB
ATTN_TILE, DIFF_UNIT_ROWS, ATTN_VMEM = 512, 256, 48 * MIB
FFN_TILE, FFN_VMEM = 1024, 56 * MIB

BF16 = jnp.bfloat16
F32 = jnp.float32


def _rms(x):
    return lax.rsqrt(jnp.mean(x * x, axis=-1, keepdims=True) + EPS)


def _layer(shape, l, **kwargs):
    zeros = (0,) * len(shape)
    return pl.BlockSpec((None,) + tuple(shape), lambda *_: (l,) + zeros, **kwargs)


def _inproj_kernel(x_ref, g_ref, w_ref, gcol_ref, cos_ref, sin_ref, bd_ref, o_ref, *, sub):
    tm = x_ref.shape[0]
    lane = lax.broadcasted_iota(jnp.int32, (sub, LANES), 1)
    lo = lane < HEAD_DIM
    first = (lane % HEAD_DIM) < HEAD_DIM // 2

    def normed(r):
        x = x_ref[r * sub:(r + 1) * sub, :]
        return (x * _rms(x) * g_ref[...]).astype(BF16)

    def twice_v(pair):
        swapped = pltpu.roll(pair, HEAD_DIM, 1)
        return jnp.concatenate([jnp.where(lo, pair, swapped), jnp.where(lo, swapped, pair)], axis=1)

    def twice_k(pair):
        quarter = HEAD_DIM // 2
        return jnp.concatenate([jnp.where(first, pair, pltpu.roll(pair, quarter, 1)),
                                jnp.where(first, pltpu.roll(pair, LANES - quarter, 1), pair)], axis=1)

    items = [(r, c) for r in range(tm // sub) for c in range(N_IN_CHUNKS)]
    hs = {0: normed(0)}

    def project(item):
        r, c = item
        return jnp.dot(hs[r], w_ref[:, c * CHUNK:(c + 1) * CHUNK], preferred_element_type=F32)

    p_next = project(items[0])
    for idx, (r, c) in enumerate(items):
        rows = slice(r * sub, (r + 1) * sub)
        cols = slice(c * CHUNK, (c + 1) * CHUNK)
        p = p_next
        if c == 0 and (r + 1) * sub < tm:
            hs[r + 1] = normed(r + 1)
        if idx + 1 < len(items):
            p_next = project(items[idx + 1])
        y = p
        if c in NORM_CHUNKS:
            ss = jnp.dot((p * p).astype(BF16), bd_ref[...], preferred_element_type=F32)
            y = p * lax.rsqrt(ss * (1.0 / HEAD_DIM) + EPS) * gcol_ref[:, cols]
            partner = jnp.concatenate(
                [pltpu.roll(y[:, g * LANES:(g + 1) * LANES], HEAD_DIM, 1) for g in range(CHUNK // LANES)],
                axis=1)
            y = y * cos_ref[rows, :] + partner * sin_ref[rows, :]
        if c == KV_CHUNK:
            o_ref[rows, SWA_Q:SWA_Q + CHUNK] = twice_k(y[:, :LANES]).astype(BF16)
            o_ref[rows, SWA_Q + CHUNK:SWA_Q + 2 * CHUNK] = twice_v(p[:, LANES:]).astype(BF16)
        else:
            out0 = c * CHUNK if c < KV_CHUNK else (c + 1) * CHUNK
            o_ref[rows, out0:out0 + CHUNK] = y.astype(BF16)


def _inproj(xf, g, w, gcol, cos_t, sin_t, bd, *, layer, seq, tm, sub):
    t = xf.shape[0]
    tiles_per_seq = seq // tm
    return pl.pallas_call(
        functools.partial(_inproj_kernel, sub=sub),
        grid=(t // tm,),
        in_specs=[
            pl.BlockSpec((tm, D_MODEL), lambda i: (i, 0)),
            _layer((1, D_MODEL), layer),
            _layer((D_MODEL, IN_COLS), layer, pipeline_mode=pl.Buffered(1)),
            _layer((1, IN_COLS), layer),
            pl.BlockSpec((tm, CHUNK), lambda i: (i % tiles_per_seq, 0)),
            pl.BlockSpec((tm, CHUNK), lambda i: (i % tiles_per_seq, 0)),
            pl.BlockSpec((CHUNK, CHUNK), lambda i: (0, 0)),
        ],
        out_specs=pl.BlockSpec((tm, PROJ_COLS), lambda i: (i, 0)),
        out_shape=jax.ShapeDtypeStruct((t, PROJ_COLS), BF16),
        compiler_params=pltpu.CompilerParams(
            dimension_semantics=("parallel",), vmem_limit_bytes=INPROJ_VMEM),
        name="inproj",
    )(xf, g, w, gcol, cos_t, sin_t, bd)


def _swa_units(sink_ref, q_ref, k_refs, v_refs, y_ref, *, layer, n, ntiles, tq):
    band = 3 * BLOCK
    blocks = tq // BLOCK
    kp_ref, kc_ref, kn_ref = k_refs
    vp_ref, vc_ref, vn_ref = v_refs
    k_ref_rows = [kp_ref] + [kc_ref] * blocks + [kn_ref]
    v_ref_rows = [vp_ref] + [vc_ref] * blocks + [vn_ref]

    def band_rows(refs, blk, lanes):
        parts = []
        for j in range(blk, blk + 3):
            ref = refs[j]
            parts.append(ref[:, lanes] if j in (0, blocks + 1)
                         else ref[(j - 1) * BLOCK:j * BLOCK, lanes])
        return jnp.concatenate(parts, axis=0)

    row = lax.broadcasted_iota(jnp.int32, (BLOCK, band), 0)
    col = lax.broadcasted_iota(jnp.int32, (BLOCK, band), 1)
    in_window = jnp.abs(col - BLOCK - row) <= WINDOW
    lane = lax.broadcasted_iota(jnp.int32, (BLOCK, LANES), 1)
    lo = lane < HEAD_DIM
    q_first = (lane % HEAD_DIM) < HEAD_DIM // 2
    q_second = jnp.logical_not(q_first)
    heads_per_kv = SWA_Q_HEADS // SWA_KV_HEADS
    ones = jnp.ones((band, LANES), BF16)
    zero = jnp.zeros((BLOCK, LANES), BF16)

    def issue(blk, kv):
        stack = []
        for hh in range(heads_per_kv):
            c0 = kv * heads_per_kv * HEAD_DIM + (hh // 2) * LANES
            qh = q_ref[blk * BLOCK:(blk + 1) * BLOCK, c0:c0 + LANES]
            stack.append(jnp.where(q_first if hh % 2 == 0 else q_second, qh, zero))
        qs = jnp.concatenate(stack, axis=0)
        kk = band_rows(k_ref_rows, blk, slice(kv * LANES, (kv + 1) * LANES))
        return lax.dot_general(qs, kk, (((1,), (1,)), ((), ())), preferred_element_type=F32)

    def finish(blk, kv, s):
        valid = in_window
        if blk == 0:
            valid = valid & ((col >= BLOCK) | (n > 0))
        if blk == blocks - 1:
            valid = valid & ((col < 2 * BLOCK) | (n < ntiles - 1))
        vv = band_rows(v_ref_rows, blk, slice(kv * LANES, (kv + 1) * LANES))
        v1 = jnp.concatenate([vv, ones], axis=1)
        s = jnp.where(valid[None], s.reshape(heads_per_kv, BLOCK, band), NEG)
        s = s.reshape(heads_per_kv * BLOCK, band)
        sk = jnp.concatenate(
            [jnp.full((BLOCK, LANES), sink_ref[layer, kv * heads_per_kv + hh] * LOG2E, F32)
             for hh in range(heads_per_kv)], axis=0)
        m = jnp.maximum(jnp.max(s, axis=-1, keepdims=True), sk)
        p = jnp.concatenate(
            [jnp.exp2(s[:, j * LANES:(j + 1) * LANES] - m) for j in range(band // LANES)],
            axis=1).astype(BF16)
        o2 = jnp.dot(p, v1, preferred_element_type=F32)
        denom = o2[:, LANES:] + jnp.exp2(sk - m)
        o = o2[:, :LANES] / denom
        for pair in range(heads_per_kv // 2):
            out = jnp.where(lo, o[2 * pair * BLOCK:(2 * pair + 1) * BLOCK],
                            o[(2 * pair + 1) * BLOCK:(2 * pair + 2) * BLOCK])
            c0 = kv * heads_per_kv * HEAD_DIM + pair * LANES
            y_ref[blk * BLOCK:(blk + 1) * BLOCK, c0:c0 + LANES] = out.astype(BF16)

    keys = [(blk, kv) for blk in range(blocks) for kv in range(SWA_KV_HEADS)]
    return [(functools.partial(issue, *key), functools.partial(finish, *key)) for key in keys]


def _diff_units(lam, subln_ref, q_ref, k_ref, v1_ref, y_ref, *, lambda_init, tq, rows):
    lo = (lax.broadcasted_iota(jnp.int32, (rows, LANES), 1) % HEAD_DIM) < HEAD_DIM // 2
    hi = jnp.logical_not(lo)
    zero = jnp.zeros((rows, LANES), BF16)
    held = {}

    def issue(h, r, c):
        qh = q_ref[r * rows:(r + 1) * rows, h * LANES:(h + 1) * LANES]
        qm = jnp.where(lo if c == 0 else hi, qh, zero)
        return lax.dot_general(qm, k_ref[:, h * LANES:(h + 1) * LANES], (((1,), (1,)), ((), ())),
                               preferred_element_type=F32)

    def finish(h, r, c, s):
        m = jnp.max(s, axis=-1, keepdims=True)
        p = jnp.exp2(s - m).astype(BF16)
        o2 = jnp.dot(p, v1_ref[h], preferred_element_type=F32)
        o = o2[:, :DIFF_V_DIM] / o2[:, DIFF_V_DIM:]
        if c == 0:
            held[(h, r)] = o
        else:
            od = held.pop((h, r)) - lam * o
            y = od * _rms(od) * subln_ref[...] * (1.0 - lambda_init)
            c0 = SWA_Q + h * DIFF_V_DIM
            y_ref[r * rows:(r + 1) * rows, c0:c0 + DIFF_V_DIM] = y.astype(BF16)

    keys = [(h, r, c) for h in range(DIFF_HEADS) for r in range(tq // rows) for c in range(2)]
    return [(functools.partial(issue, *key), functools.partial(finish, *key)) for key in keys]


def _attn_kernel(sink_ref, lq1_ref, lk1_ref, lq2_ref, lk2_ref, subln_ref,
                 qa_ref, kp_ref, kc_ref, kn_ref, vp_ref, vc_ref, vn_ref, qb_ref, kb_ref, vb_ref,
                 x_ref, wo_ref, o_ref, v1_ref, y_ref, *, layer, lambda_init, ntiles, tq, rows):
    n = pl.program_id(1)
    lam = (jnp.exp(jnp.sum(lq1_ref[...] * lk1_ref[...], axis=-1, keepdims=True))
           - jnp.exp(jnp.sum(lq2_ref[...] * lk2_ref[...], axis=-1, keepdims=True))
           + lambda_init)

    @pl.when(n == 0)
    def _():
        for h in range(DIFF_HEADS):
            v1_ref[h, :, :DIFF_V_DIM] = vb_ref[:, h * DIFF_V_DIM:(h + 1) * DIFF_V_DIM]
            v1_ref[h, :, DIFF_V_DIM:] = jnp.ones((vb_ref.shape[0], DIFF_V_DIM), BF16)

    units = _swa_units(sink_ref, qa_ref, (kp_ref, kc_ref, kn_ref), (vp_ref, vc_ref, vn_ref), y_ref,
                       layer=layer, n=n, ntiles=ntiles, tq=tq)
    units += _diff_units(lam, subln_ref, qb_ref, kb_ref, v1_ref, y_ref,
                         lambda_init=lambda_init, tq=tq, rows=rows)
    s_next = units[0][0]()
    for idx, (_, finish) in enumerate(units):
        s = s_next
        if idx + 1 < len(units):
            s_next = units[idx + 1][0]()
        finish(s)
    o_ref[...] = x_ref[...] + jnp.dot(y_ref[...], wo_ref[...], preferred_element_type=F32)


def _attn(proj3, x3, sink, lq1, lk1, lq2, lk2, subln, wo, *, layer, lambda_init, tq, rows):
    b, s, _ = proj3.shape
    ntiles = s // tq
    per_tile = tq // BLOCK
    last_block = s // BLOCK - 1
    kcol = SWA_Q // CHUNK
    qcol = (2 * SWA_Q) // DIFF_Q
    prev = lambda col: pl.BlockSpec(
        (None, BLOCK, CHUNK), lambda bi, n: (bi, jnp.maximum(n * per_tile - 1, 0), col))
    own = lambda col: pl.BlockSpec((None, tq, CHUNK), lambda bi, n: (bi, n, col))
    nxt = lambda col: pl.BlockSpec(
        (None, BLOCK, CHUNK), lambda bi, n: (bi, jnp.minimum((n + 1) * per_tile, last_block), col))
    return pl.pallas_call(
        functools.partial(_attn_kernel, layer=layer, lambda_init=lambda_init, ntiles=ntiles, tq=tq,
                          rows=rows),
        grid=(b, ntiles),
        in_specs=[
            pl.BlockSpec(memory_space=pltpu.SMEM),
            _layer((1, HEAD_DIM), layer), _layer((1, HEAD_DIM), layer),
            _layer((1, HEAD_DIM), layer), _layer((1, HEAD_DIM), layer),
            _layer((1, DIFF_V_DIM), layer),
            pl.BlockSpec((None, tq, SWA_Q), lambda bi, n: (bi, n, 0)),
            prev(kcol), own(kcol), nxt(kcol),
            prev(kcol + 1), own(kcol + 1), nxt(kcol + 1),
            pl.BlockSpec((None, tq, DIFF_Q), lambda bi, n: (bi, n, qcol)),
            pl.BlockSpec((None, s, DIFF_Q), lambda bi, n: (bi, 0, qcol + 1)),
            pl.BlockSpec((None, s, DIFF_V), lambda bi, n: (bi, 0, qcol + 2)),
            pl.BlockSpec((None, tq, D_MODEL), lambda bi, n: (bi, n, 0)),
            _layer((SWA_Q + DIFF_V, D_MODEL), layer),
        ],
        out_specs=pl.BlockSpec((None, tq, D_MODEL), lambda bi, n: (bi, n, 0)),
        out_shape=jax.ShapeDtypeStruct((b, s, D_MODEL), F32),
        scratch_shapes=[pltpu.VMEM((DIFF_HEADS, s, 2 * DIFF_V_DIM), BF16),
                        pltpu.VMEM((tq, SWA_Q + DIFF_V), BF16)],
        compiler_params=pltpu.CompilerParams(
            dimension_semantics=("parallel", "arbitrary"), vmem_limit_bytes=ATTN_VMEM),
        name="attn",
    )(sink, lq1, lk1, lq2, lk2, subln, proj3, proj3, proj3, proj3, proj3, proj3, proj3,
      proj3, proj3, proj3, x3, wo)


def _ffn_kernel(xp_ref, xc_ref, xn_ref, g_ref, wu_ref, cw_ref, cb_ref, wd_ref, o_ref,
                hext_ref, *, ts, tiles_per_seq):
    j = pl.program_id(0) % tiles_per_seq
    g = g_ref[...]

    def norm(xx):
        return (xx * _rms(xx) * g).astype(BF16)

    hp = norm(xp_ref[...])
    hn = norm(xn_ref[...])
    hext_ref[0:HALO, :] = jnp.where(j > 0, hp, jnp.zeros_like(hp))
    hext_ref[HALO:HALO + ts, :] = norm(xc_ref[...])
    hext_ref[HALO + ts:, :] = jnp.where(j < tiles_per_seq - 1, hn, jnp.zeros_like(hn))

    chunks = [slice(c0, c0 + FF_CHUNK) for c0 in range(0, D_FF, FF_CHUNK)]

    def up(cols):
        gate = jnp.dot(hext_ref[...], wu_ref[:, cols], preferred_element_type=F32)
        val = jnp.dot(hext_ref[HALO:HALO + ts, :],
                      wu_ref[:, D_FF + cols.start:D_FF + cols.stop], preferred_element_type=F32)
        return gate, val

    ahead = up(chunks[0])
    acts = []
    for c, cols in enumerate(chunks):
        gate, val = ahead
        if c + 1 < len(chunks):
            ahead = up(chunks[c + 1])
        cw = cw_ref[:, cols]
        rows = gate.shape[0]
        g_prev = pltpu.roll(gate, 1, 0)[HALO:HALO + ts]
        g_next = pltpu.roll(gate, rows - 1, 0)[HALO:HALO + ts]
        g_cur = gate[HALO:HALO + ts]
        conv = cb_ref[:, cols] + cw[0:1] * g_prev + cw[1:2] * g_cur + cw[2:3] * g_next
        acts.append((conv * (1.0 / (1.0 + jnp.exp(-conv))) * val).astype(BF16))
    o_ref[...] = xc_ref[...] + jnp.dot(jnp.concatenate(acts, axis=1), wd_ref[...],
                                       preferred_element_type=F32)


def _ffn(xf, g, wu, cw, cb, wd, *, layer, seq, ts):
    t = xf.shape[0]
    tiles_per_seq = seq // ts
    halo_blocks = ts // HALO
    last_halo = t // HALO - 1
    resident = pl.Buffered(1)
    return pl.pallas_call(
        functools.partial(_ffn_kernel, ts=ts, tiles_per_seq=tiles_per_seq),
        grid=(t // ts,),
        in_specs=[
            pl.BlockSpec((HALO, D_MODEL), lambda i: (jnp.maximum(i * halo_blocks - 1, 0), 0)),
            pl.BlockSpec((ts, D_MODEL), lambda i: (i, 0)),
            pl.BlockSpec((HALO, D_MODEL), lambda i: (jnp.minimum((i + 1) * halo_blocks, last_halo), 0)),
            _layer((1, D_MODEL), layer),
            _layer((D_MODEL, 2 * D_FF), layer, pipeline_mode=resident),
            _layer((3, D_FF), layer),
            _layer((1, D_FF), layer),
            _layer((D_FF, D_MODEL), layer, pipeline_mode=resident),
        ],
        out_specs=pl.BlockSpec((ts, D_MODEL), lambda i: (i, 0)),
        out_shape=jax.ShapeDtypeStruct((t, D_MODEL), F32),
        scratch_shapes=[pltpu.VMEM((ts + 2 * HALO, D_MODEL), BF16)],
        compiler_params=pltpu.CompilerParams(
            dimension_semantics=("parallel",), vmem_limit_bytes=FFN_VMEM),
        name="convglu",
    )(xf, xf, xf, g, wu, cw, cb, wd)


def _rope_tables(seq):
    inv = 1.0 / (ROPE_THETA ** (np.arange(0, HEAD_DIM, 2, dtype=np.float64) / HEAD_DIM))
    ang = np.arange(seq, dtype=np.float64)[:, None] * inv[None, :]
    cos, sin = np.cos(ang), np.sin(ang)
    reps = CHUNK // LANES
    cos_t = np.tile(np.concatenate([cos, cos, cos, cos], axis=-1), (1, reps))
    sin_t = np.tile(np.concatenate([-sin, -sin, sin, sin], axis=-1), (1, reps))
    return jnp.asarray(cos_t, F32), jnp.asarray(sin_t, F32)


def _head_block_diag():
    lane = np.arange(CHUNK)
    head = (lane // LANES) * 2 + (lane // (HEAD_DIM // 2)) % 2
    return jnp.asarray(head[:, None] == head[None, :], BF16)


def _pair_layout(a):
    lead, n = a.shape[:-1], a.shape[-1]
    a = a.reshape(lead + (n // LANES, 2, 2, HEAD_DIM // 2))
    return jnp.swapaxes(a, -3, -2).reshape(lead + (n,))


def kernel(x, g_attn, w_in, qn_a, kn_a, sink, qn_b, kn_b, lq1, lk1, lq2, lk2, subln, w_out, g_ffn,
           w_up, conv_w, conv_b, w_down):
    b, s, d = x.shape
    depth = w_in.shape[0]
    cos_t, sin_t = _rope_tables(s)
    bd = _head_block_diag()
    scale = HEAD_DIM ** -0.5 * LOG2E
    xf = x.reshape(b * s, d)

    row = lambda p: p[:, None, :]
    heads = lambda g, n: jnp.tile(g, (1, n))
    kv0, kv1 = SWA_Q + SWA_KV, SWA_Q + 2 * SWA_KV
    w_bf = w_in.astype(BF16)
    w_proj = jnp.concatenate([
        _pair_layout(w_bf[..., :kv0]), w_bf[..., kv0:kv1],
        _pair_layout(w_bf[..., kv1:kv1 + 2 * DIFF_Q]), w_bf[..., kv1 + 2 * DIFF_Q:]], axis=-1)
    gcol = row(jnp.concatenate([
        _pair_layout(jnp.concatenate([heads(qn_a, SWA_Q_HEADS) * scale, heads(kn_a, SWA_KV_HEADS)], -1)),
        jnp.ones((depth, SWA_KV), F32),
        _pair_layout(jnp.concatenate([heads(qn_b, 2 * DIFF_HEADS) * scale,
                                      heads(kn_b, 2 * DIFF_HEADS)], -1)),
        jnp.ones((depth, DIFF_V), F32)], axis=-1))
    wo = w_out.astype(BF16)
    wu = w_up.astype(BF16)
    wd = w_down.astype(BF16)
    g_attn3, g_ffn3, subln3, conv_b3 = row(g_attn), row(g_ffn), row(subln), row(conv_b)
    lq1_3, lk1_3, lq2_3, lk2_3 = row(lq1), row(lk1), row(lq2), row(lk2)

    for l in range(depth):
        lambda_init = 0.8 - 0.6 * math.exp(-0.3 * l)
        proj = _inproj(xf, g_attn3, w_proj, gcol, cos_t, sin_t, bd, layer=l, seq=s,
                       tm=INPROJ_TILE, sub=INPROJ_SUBTILE)
        proj3 = proj.reshape(b, s, PROJ_COLS)
        x3 = _attn(proj3, xf.reshape(b, s, d), sink, lq1_3, lk1_3, lq2_3, lk2_3, subln3, wo,
                   layer=l, lambda_init=lambda_init, tq=ATTN_TILE, rows=DIFF_UNIT_ROWS)
        xf = _ffn(x3.reshape(b * s, d), g_ffn3, wu, conv_w, conv_b3, wd, layer=l, seq=s, ts=FFN_TILE)
    return xf.reshape(b, s, d)
```

```python
import functools
import math

import jax
import jax.numpy as jnp
import numpy as np
from jax import lax
from jax.experimental import pallas as pl
from jax.experimental.pallas import tpu as pltpu

D_MODEL = 1024
HEAD_DIM = 64
SWA_Q_HEADS = 8
SWA_KV_HEADS = 2
WINDOW = 128
BLOCK = 128
DIFF_HEADS = 4
DIFF_V_DIM = 128
D_FF = 2816
ROPE_THETA = 10000.0
EPS = 1e-6
NEG = -1e30
LOG2E = 1.4426950408889634

SWA_Q = SWA_Q_HEADS * HEAD_DIM
SWA_KV = SWA_KV_HEADS * HEAD_DIM
DIFF_Q = DIFF_HEADS * 2 * HEAD_DIM
DIFF_V = DIFF_HEADS * DIFF_V_DIM

LANES = 128
MXU_COLS = 256
CHUNK = MXU_COLS
PROJ_COLS = 2560
IN_COLS = SWA_Q + 2 * SWA_KV + 2 * DIFF_Q + DIFF_V
N_IN_CHUNKS = IN_COLS // CHUNK
KV_CHUNK = SWA_Q // CHUNK
NORM_CHUNKS = (0, 1, 2, 3, 4, 5, 6)
FF_CHUNK = MXU_COLS
HALO = 16

MIB = 1024 * 1024
INPROJ_TILE, INPROJ_SUBTILE, INPROJ_VMEM = 1024, 512, 48 * MIB
ATTN_TILE, DIFF_UNIT_ROWS, ATTN_VMEM = 512, 256, 48 * MIB
FFN_TILE, FFN_VMEM = 1024, 56 * MIB

BF16 = jnp.bfloat16
F32 = jnp.float32


def _rms(x):
    return lax.rsqrt(jnp.mean(x * x, axis=-1, keepdims=True) + EPS)


def _layer(shape, l, **kwargs):
    zeros = (0,) * len(shape)
    return pl.BlockSpec((None,) + tuple(shape), lambda *_: (l,) + zeros, **kwargs)


def _inproj_kernel(x_ref, g_ref, w_ref, gcol_ref, cos_ref, sin_ref, bd_ref, o_ref, *, sub):
    tm = x_ref.shape[0]
    lane = lax.broadcasted_iota(jnp.int32, (sub, CHUNK), 1)
    first_half = (lane % HEAD_DIM) < (HEAD_DIM // 2)
    lo = lax.broadcasted_iota(jnp.int32, (sub, LANES), 1) < HEAD_DIM

    def normed(r):
        x = x_ref[r * sub:(r + 1) * sub, :]
        return (x * _rms(x) * g_ref[...]).astype(BF16)

    def twice(pair):
        swapped = pltpu.roll(pair, HEAD_DIM, 1)
        return jnp.concatenate([jnp.where(lo, pair, swapped), jnp.where(lo, swapped, pair)], axis=1)

    items = [(r, c) for r in range(tm // sub) for c in range(N_IN_CHUNKS)]
    hs = {0: normed(0)}

    def project(item):
        r, c = item
        return jnp.dot(hs[r], w_ref[:, c * CHUNK:(c + 1) * CHUNK], preferred_element_type=F32)

    p_next = project(items[0])
    for idx, (r, c) in enumerate(items):
        rows = slice(r * sub, (r + 1) * sub)
        cols = slice(c * CHUNK, (c + 1) * CHUNK)
        p = p_next
        if c == 0 and (r + 1) * sub < tm:
            hs[r + 1] = normed(r + 1)
        if idx + 1 < len(items):
            p_next = project(items[idx + 1])
        y = p
        if c in NORM_CHUNKS:
            ss = jnp.dot((p * p).astype(BF16), bd_ref[...], preferred_element_type=F32)
            y = p * gcol_ref[:, cols]
            partner = jnp.where(first_half,
                                pltpu.roll(y, CHUNK - HEAD_DIM // 2, 1),
                                pltpu.roll(y, HEAD_DIM // 2, 1))
            y = (y * cos_ref[rows, :] + partner * sin_ref[rows, :]) * lax.rsqrt(ss * (1.0 / HEAD_DIM) + EPS)
        if c == KV_CHUNK:
            o_ref[rows, SWA_Q:SWA_Q + CHUNK] = twice(y[:, :LANES]).astype(BF16)
            o_ref[rows, SWA_Q + CHUNK:SWA_Q + 2 * CHUNK] = twice(p[:, LANES:]).astype(BF16)
        else:
            out0 = c * CHUNK if c < KV_CHUNK else (c + 1) * CHUNK
            o_ref[rows, out0:out0 + CHUNK] = y.astype(BF16)


def _inproj(xf, g, w, gcol, cos_t, sin_t, bd, *, layer, seq, tm, sub):
    t = xf.shape[0]
    tiles_per_seq = seq // tm
    return pl.pallas_call(
        functools.partial(_inproj_kernel, sub=sub),
        grid=(t // tm,),
        in_specs=[
            pl.BlockSpec((tm, D_MODEL), lambda i: (i, 0)),
            _layer((1, D_MODEL), layer),
            _layer((D_MODEL, IN_COLS), layer, pipeline_mode=pl.Buffered(1)),
            _layer((1, IN_COLS), layer),
            pl.BlockSpec((tm, CHUNK), lambda i: (i % tiles_per_seq, 0)),
            pl.BlockSpec((tm, CHUNK), lambda i: (i % tiles_per_seq, 0)),
            pl.BlockSpec((CHUNK, CHUNK), lambda i: (0, 0)),
        ],
        out_specs=pl.BlockSpec((tm, PROJ_COLS), lambda i: (i, 0)),
        out_shape=jax.ShapeDtypeStruct((t, PROJ_COLS), BF16),
        compiler_params=pltpu.CompilerParams(
            dimension_semantics=("parallel",), vmem_limit_bytes=INPROJ_VMEM),
        name="inproj",
    )(xf, g, w, gcol, cos_t, sin_t, bd)


def _swa_units(sink_ref, q_ref, k_refs, v_refs, y_ref, *, layer, n, ntiles, tq):
    band = 3 * BLOCK
    blocks = tq // BLOCK
    kp_ref, kc_ref, kn_ref = k_refs
    vp_ref, vc_ref, vn_ref = v_refs
    k_ref_rows = [kp_ref] + [kc_ref] * blocks + [kn_ref]
    v_ref_rows = [vp_ref] + [vc_ref] * blocks + [vn_ref]

    def band_rows(refs, blk, lanes):
        parts = []
        for j in range(blk, blk + 3):
            ref = refs[j]
            parts.append(ref[:, lanes] if j in (0, blocks + 1)
                         else ref[(j - 1) * BLOCK:j * BLOCK, lanes])
        return jnp.concatenate(parts, axis=0)

    row = lax.broadcasted_iota(jnp.int32, (BLOCK, band), 0)
    col = lax.broadcasted_iota(jnp.int32, (BLOCK, band), 1)
    in_window = jnp.abs(col - BLOCK - row) <= WINDOW
    lo = lax.broadcasted_iota(jnp.int32, (BLOCK, LANES), 1) < HEAD_DIM
    hi = jnp.logical_not(lo)
    heads_per_kv = SWA_Q_HEADS // SWA_KV_HEADS
    ones = jnp.ones((band, LANES), BF16)
    zero = jnp.zeros((BLOCK, LANES), BF16)

    def issue(blk, kv):
        stack = []
        for hh in range(heads_per_kv):
            c0 = kv * heads_per_kv * HEAD_DIM + (hh // 2) * LANES
            qh = q_ref[blk * BLOCK:(blk + 1) * BLOCK, c0:c0 + LANES]
            stack.append(jnp.where(lo if hh % 2 == 0 else hi, qh, zero))
        qs = jnp.concatenate(stack, axis=0)
        kk = band_rows(k_ref_rows, blk, slice(kv * LANES, (kv + 1) * LANES))
        return lax.dot_general(qs, kk, (((1,), (1,)), ((), ())), preferred_element_type=F32)

    def finish(blk, kv, s):
        valid = in_window
        if blk == 0:
            valid = valid & ((col >= BLOCK) | (n > 0))
        if blk == blocks - 1:
            valid = valid & ((col < 2 * BLOCK) | (n < ntiles - 1))
        vv = band_rows(v_ref_rows, blk, slice(kv * LANES, (kv + 1) * LANES))
        v1 = jnp.concatenate([vv, ones], axis=1)
        s = jnp.where(valid[None], s.reshape(heads_per_kv, BLOCK, band), NEG)
        s = s.reshape(heads_per_kv * BLOCK, band)
        sk = jnp.concatenate(
            [jnp.full((BLOCK, LANES), sink_ref[layer, kv * heads_per_kv + hh] * LOG2E, F32)
             for hh in range(heads_per_kv)], axis=0)
        m = jnp.maximum(jnp.max(s, axis=-1, keepdims=True), sk)
        p = jnp.concatenate(
            [jnp.exp2(s[:, j * LANES:(j + 1) * LANES] - m) for j in range(band // LANES)],
            axis=1).astype(BF16)
        o2 = jnp.dot(p, v1, preferred_element_type=F32)
        denom = o2[:, LANES:] + jnp.exp2(sk - m)
        o = o2[:, :LANES] / denom
        for pair in range(heads_per_kv // 2):
            out = jnp.where(lo, o[2 * pair * BLOCK:(2 * pair + 1) * BLOCK],
                            o[(2 * pair + 1) * BLOCK:(2 * pair + 2) * BLOCK])
            c0 = kv * heads_per_kv * HEAD_DIM + pair * LANES
            y_ref[blk * BLOCK:(blk + 1) * BLOCK, c0:c0 + LANES] = out.astype(BF16)

    keys = [(blk, kv) for blk in range(blocks) for kv in range(SWA_KV_HEADS)]
    return [(functools.partial(issue, *key), functools.partial(finish, *key)) for key in keys]


def _diff_units(lam, subln_ref, q_ref, k_ref, v1_ref, y_ref, *, lambda_init, tq, rows):
    lo = lax.broadcasted_iota(jnp.int32, (rows, LANES), 1) < HEAD_DIM
    hi = jnp.logical_not(lo)
    zero = jnp.zeros((rows, LANES), BF16)
    held = {}

    def issue(h, r, c):
        qh = q_ref[r * rows:(r + 1) * rows, h * LANES:(h + 1) * LANES]
        qm = jnp.where(lo if c == 0 else hi, qh, zero)
        return lax.dot_general(qm, k_ref[:, h * LANES:(h + 1) * LANES], (((1,), (1,)), ((), ())),
                               preferred_element_type=F32)

    def finish(h, r, c, s):
        m = jnp.max(s, axis=-1, keepdims=True)
        p = jnp.exp2(s - m).astype(BF16)
        o2 = jnp.dot(p, v1_ref[h], preferred_element_type=F32)
        o = o2[:, :DIFF_V_DIM] / o2[:, DIFF_V_DIM:]
        if c == 0:
            held[(h, r)] = o
        else:
            od = held.pop((h, r)) - lam * o
            y = od * _rms(od) * subln_ref[...] * (1.0 - lambda_init)
            c0 = SWA_Q + h * DIFF_V_DIM
            y_ref[r * rows:(r + 1) * rows, c0:c0 + DIFF_V_DIM] = y.astype(BF16)

    keys = [(h, r, c) for h in range(DIFF_HEADS) for r in range(tq // rows) for c in range(2)]
    return [(functools.partial(issue, *key), functools.partial(finish, *key)) for key in keys]


def _attn_kernel(sink_ref, lq1_ref, lk1_ref, lq2_ref, lk2_ref, subln_ref,
                 qa_ref, kp_ref, kc_ref, kn_ref, vp_ref, vc_ref, vn_ref, qb_ref, kb_ref, vb_ref,
                 x_ref, wo_ref, o_ref, v1_ref, y_ref, *, layer, lambda_init, ntiles, tq, rows):
    n = pl.program_id(1)
    lam = (jnp.exp(jnp.sum(lq1_ref[...] * lk1_ref[...], axis=-1, keepdims=True))
           - jnp.exp(jnp.sum(lq2_ref[...] * lk2_ref[...], axis=-1, keepdims=True))
           + lambda_init)

    @pl.when(n == 0)
    def _():
        for h in range(DIFF_HEADS):
            v1_ref[h, :, :DIFF_V_DIM] = vb_ref[:, h * DIFF_V_DIM:(h + 1) * DIFF_V_DIM]
            v1_ref[h, :, DIFF_V_DIM:] = jnp.ones((vb_ref.shape[0], DIFF_V_DIM), BF16)

    units = _swa_units(sink_ref, qa_ref, (kp_ref, kc_ref, kn_ref), (vp_ref, vc_ref, vn_ref), y_ref,
                       layer=layer, n=n, ntiles=ntiles, tq=tq)
    units += _diff_units(lam, subln_ref, qb_ref, kb_ref, v1_ref, y_ref,
                         lambda_init=lambda_init, tq=tq, rows=rows)
    s_next = units[0][0]()
    for idx, (_, finish) in enumerate(units):
        s = s_next
        if idx + 1 < len(units):
            s_next = units[idx + 1][0]()
        finish(s)
    o_ref[...] = x_ref[...] + jnp.dot(y_ref[...], wo_ref[...], preferred_element_type=F32)


def _attn(proj3, x3, sink, lq1, lk1, lq2, lk2, subln, wo, *, layer, lambda_init, tq, rows):
    b, s, _ = proj3.shape
    ntiles = s // tq
    per_tile = tq // BLOCK
    last_block = s // BLOCK - 1
    kcol = SWA_Q // CHUNK
    qcol = (2 * SWA_Q) // DIFF_Q
    prev = lambda col: pl.BlockSpec(
        (None, BLOCK, CHUNK), lambda bi, n: (bi, jnp.maximum(n * per_tile - 1, 0), col))
    own = lambda col: pl.BlockSpec((None, tq, CHUNK), lambda bi, n: (bi, n, col))
    nxt = lambda col: pl.BlockSpec(
        (None, BLOCK, CHUNK), lambda bi, n: (bi, jnp.minimum((n + 1) * per_tile, last_block), col))
    return pl.pallas_call(
        functools.partial(_attn_kernel, layer=layer, lambda_init=lambda_init, ntiles=ntiles, tq=tq,
                          rows=rows),
        grid=(b, ntiles),
        in_specs=[
            pl.BlockSpec(memory_space=pltpu.SMEM),
            _layer((1, HEAD_DIM), layer), _layer((1, HEAD_DIM), layer),
            _layer((1, HEAD_DIM), layer), _layer((1, HEAD_DIM), layer),
            _layer((1, DIFF_V_DIM), layer),
            pl.BlockSpec((None, tq, SWA_Q), lambda bi, n: (bi, n, 0)),
            prev(kcol), own(kcol), nxt(kcol),
            prev(kcol + 1), own(kcol + 1), nxt(kcol + 1),
            pl.BlockSpec((None, tq, DIFF_Q), lambda bi, n: (bi, n, qcol)),
            pl.BlockSpec((None, s, DIFF_Q), lambda bi, n: (bi, 0, qcol + 1)),
            pl.BlockSpec((None, s, DIFF_V), lambda bi, n: (bi, 0, qcol + 2)),
            pl.BlockSpec((None, tq, D_MODEL), lambda bi, n: (bi, n, 0)),
            _layer((SWA_Q + DIFF_V, D_MODEL), layer),
        ],
        out_specs=pl.BlockSpec((None, tq, D_MODEL), lambda bi, n: (bi, n, 0)),
        out_shape=jax.ShapeDtypeStruct((b, s, D_MODEL), F32),
        scratch_shapes=[pltpu.VMEM((DIFF_HEADS, s, 2 * DIFF_V_DIM), BF16),
                        pltpu.VMEM((tq, SWA_Q + DIFF_V), BF16)],
        compiler_params=pltpu.CompilerParams(
            dimension_semantics=("parallel", "arbitrary"), vmem_limit_bytes=ATTN_VMEM),
        name="attn",
    )(sink, lq1, lk1, lq2, lk2, subln, proj3, proj3, proj3, proj3, proj3, proj3, proj3,
      proj3, proj3, proj3, x3, wo)


def _ffn_kernel(xp_ref, xc_ref, xn_ref, g_ref, wu_ref, cw_ref, cb_ref, wd_ref, o_ref,
                hext_ref, *, ts, tiles_per_seq):
    j = pl.program_id(0) % tiles_per_seq
    g = g_ref[...]

    def norm(xx):
        return (xx * _rms(xx) * g).astype(BF16)

    hp = norm(xp_ref[...])
    hn = norm(xn_ref[...])
    hext_ref[0:HALO, :] = jnp.where(j > 0, hp, jnp.zeros_like(hp))
    hext_ref[HALO:HALO + ts, :] = norm(xc_ref[...])
    hext_ref[HALO + ts:, :] = jnp.where(j < tiles_per_seq - 1, hn, jnp.zeros_like(hn))

    chunks = [slice(c0, c0 + FF_CHUNK) for c0 in range(0, D_FF, FF_CHUNK)]

    def up(cols):
        gate = jnp.dot(hext_ref[...], wu_ref[:, cols], preferred_element_type=F32)
        val = jnp.dot(hext_ref[HALO:HALO + ts, :],
                      wu_ref[:, D_FF + cols.start:D_FF + cols.stop], preferred_element_type=F32)
        return gate, val

    ahead = up(chunks[0])
    acts = []
    for c, cols in enumerate(chunks):
        gate, val = ahead
        if c + 1 < len(chunks):
            ahead = up(chunks[c + 1])
        cw = cw_ref[:, cols]
        rows = gate.shape[0]
        g_prev = pltpu.roll(gate, 1, 0)[HALO:HALO + ts]
        g_next = pltpu.roll(gate, rows - 1, 0)[HALO:HALO + ts]
        g_cur = gate[HALO:HALO + ts]
        conv = cb_ref[:, cols] + cw[0:1] * g_prev + cw[1:2] * g_cur + cw[2:3] * g_next
        acts.append((conv * (1.0 / (1.0 + jnp.exp(-conv))) * val).astype(BF16))
    o_ref[...] = xc_ref[...] + jnp.dot(jnp.concatenate(acts, axis=1), wd_ref[...],
                                       preferred_element_type=F32)


def _ffn(xf, g, wu, cw, cb, wd, *, layer, seq, ts):
    t = xf.shape[0]
    tiles_per_seq = seq // ts
    halo_blocks = ts // HALO
    last_halo = t // HALO - 1
    resident = pl.Buffered(1)
    return pl.pallas_call(
        functools.partial(_ffn_kernel, ts=ts, tiles_per_seq=tiles_per_seq),
        grid=(t // ts,),
        in_specs=[
            pl.BlockSpec((HALO, D_MODEL), lambda i: (jnp.maximum(i * halo_blocks - 1, 0), 0)),
            pl.BlockSpec((ts, D_MODEL), lambda i: (i, 0)),
            pl.BlockSpec((HALO, D_MODEL), lambda i: (jnp.minimum((i + 1) * halo_blocks, last_halo), 0)),
            _layer((1, D_MODEL), layer),
            _layer((D_MODEL, 2 * D_FF), layer, pipeline_mode=resident),
            _layer((3, D_FF), layer),
            _layer((1, D_FF), layer),
            _layer((D_FF, D_MODEL), layer, pipeline_mode=resident),
        ],
        out_specs=pl.BlockSpec((ts, D_MODEL), lambda i: (i, 0)),
        out_shape=jax.ShapeDtypeStruct((t, D_MODEL), F32),
        scratch_shapes=[pltpu.VMEM((ts + 2 * HALO, D_MODEL), BF16)],
        compiler_params=pltpu.CompilerParams(
            dimension_semantics=("parallel",), vmem_limit_bytes=FFN_VMEM),
        name="convglu",
    )(xf, xf, xf, g, wu, cw, cb, wd)


def _rope_tables(seq):
    inv = 1.0 / (ROPE_THETA ** (np.arange(0, HEAD_DIM, 2, dtype=np.float64) / HEAD_DIM))
    ang = np.arange(seq, dtype=np.float64)[:, None] * inv[None, :]
    cos, sin = np.cos(ang), np.sin(ang)
    reps = CHUNK // HEAD_DIM
    cos_t = np.tile(np.concatenate([cos, cos], axis=-1), (1, reps))
    sin_t = np.tile(np.concatenate([-sin, sin], axis=-1), (1, reps))
    return jnp.asarray(cos_t, F32), jnp.asarray(sin_t, F32)


def _head_block_diag():
    idx = np.arange(CHUNK) // HEAD_DIM
    return jnp.asarray(idx[:, None] == idx[None, :], BF16)


def kernel(x, g_attn, w_in, qn_a, kn_a, sink, qn_b, kn_b, lq1, lk1, lq2, lk2, subln, w_out, g_ffn,
           w_up, conv_w, conv_b, w_down):
    b, s, d = x.shape
    depth = w_in.shape[0]
    cos_t, sin_t = _rope_tables(s)
    bd = _head_block_diag()
    scale = HEAD_DIM ** -0.5 * LOG2E
    xf = x.reshape(b * s, d)

    row = lambda p: p[:, None, :]
    heads = lambda g, n: jnp.tile(g, (1, n))
    w_proj = w_in.astype(BF16)
    gcol = row(jnp.concatenate([
        heads(qn_a, SWA_Q_HEADS) * scale, heads(kn_a, SWA_KV_HEADS),
        jnp.ones((depth, SWA_KV), F32),
        heads(qn_b, 2 * DIFF_HEADS) * scale, heads(kn_b, 2 * DIFF_HEADS),
        jnp.ones((depth, DIFF_V), F32)], axis=-1))
    wo = w_out.astype(BF16)
    wu = w_up.astype(BF16)
    wd = w_down.astype(BF16)
    g_attn3, g_ffn3, subln3, conv_b3 = row(g_attn), row(g_ffn), row(subln), row(conv_b)
    lq1_3, lk1_3, lq2_3, lk2_3 = row(lq1), row(lk1), row(lq2), row(lk2)

    for l in range(depth):
        lambda_init = 0.8 - 0.6 * math.exp(-0.3 * l)
        proj = _inproj(xf, g_attn3, w_proj, gcol, cos_t, sin_t, bd, layer=l, seq=s,
                       tm=INPROJ_TILE, sub=INPROJ_SUBTILE)
        proj3 = proj.reshape(b, s, PROJ_COLS)
        x3 = _attn(proj3, xf.reshape(b, s, d), sink, lq1_3, lk1_3, lq2_3, lk2_3, subln3, wo,
                   layer=l, lambda_init=lambda_init, tq=ATTN_TILE, rows=DIFF_UNIT_ROWS)
        xf = _ffn(x3.reshape(b * s, d), g_ffn3, wu, conv_w, conv_b3, wd, layer=l, seq=s, ts=FFN_TILE)
    return xf.reshape(b, s, d)
```

```python
import functools
import math

import jax
import jax.numpy as jnp
import numpy as np
from jax import lax
from jax.experimental import pallas as pl
from jax.experimental.pallas import tpu as pltpu

D_MODEL = 1024
HEAD_DIM = 64
SWA_Q_HEADS = 8
SWA_KV_HEADS = 2
WINDOW = 128
BLOCK = 128
DIFF_HEADS = 4
DIFF_V_DIM = 128
D_FF = 2816
ROPE_THETA = 10000.0
EPS = 1e-6
NEG = -1e30
LOG2E = 1.4426950408889634

SWA_Q = SWA_Q_HEADS * HEAD_DIM
SWA_KV = SWA_KV_HEADS * HEAD_DIM
DIFF_Q = DIFF_HEADS * 2 * HEAD_DIM
DIFF_V = DIFF_HEADS * DIFF_V_DIM

LANES = 128
MXU_COLS = 256
CHUNK = MXU_COLS
PROJ_COLS = 2560
IN_COLS = SWA_Q + 2 * SWA_KV + 2 * DIFF_Q + DIFF_V
N_IN_CHUNKS = IN_COLS // CHUNK
KV_CHUNK = SWA_Q // CHUNK
NORM_CHUNKS = (0, 1, 2, 3, 4, 5, 6)
FF_CHUNK = MXU_COLS
HALO = 16
EPILOGUE_ROWS = 64

MIB = 1024 * 1024
INPROJ_TILE, INPROJ_SUBTILE, INPROJ_VMEM = 1024, 512, 48 * MIB
ATTN_TILE, DIFF_UNIT_ROWS, ATTN_VMEM = 512, 256, 48 * MIB
FFN_TILE, FFN_VMEM = 1024, 56 * MIB

BF16 = jnp.bfloat16
F32 = jnp.float32


def _rms(x):
    return lax.rsqrt(jnp.mean(x * x, axis=-1, keepdims=True) + EPS)


def _layer(shape, l, **kwargs):
    zeros = (0,) * len(shape)
    return pl.BlockSpec((None,) + tuple(shape), lambda *_: (l,) + zeros, **kwargs)


def _inproj_kernel(x_ref, g_ref, w_ref, gcol_ref, cos_ref, sin_ref, bd_ref, o_ref, *, sub):
    tm = x_ref.shape[0]
    lane = lax.broadcasted_iota(jnp.int32, (EPILOGUE_ROWS, CHUNK), 1)
    first_half = (lane % HEAD_DIM) < (HEAD_DIM // 2)
    lo = lax.broadcasted_iota(jnp.int32, (sub, LANES), 1) < HEAD_DIM

    def normed(r):
        x = x_ref[r * sub:(r + 1) * sub, :]
        return (x * _rms(x) * g_ref[...]).astype(BF16)

    def twice(pair):
        swapped = pltpu.roll(pair, HEAD_DIM, 1)
        return jnp.concatenate([jnp.where(lo, pair, swapped), jnp.where(lo, swapped, pair)], axis=1)

    items = [(r, c) for r in range(tm // sub) for c in range(N_IN_CHUNKS)]
    hs = {0: normed(0)}

    def project(item):
        r, c = item
        return jnp.dot(hs[r], w_ref[:, c * CHUNK:(c + 1) * CHUNK], preferred_element_type=F32)

    p_next = project(items[0])
    for idx, (r, c) in enumerate(items):
        rows = slice(r * sub, (r + 1) * sub)
        cols = slice(c * CHUNK, (c + 1) * CHUNK)
        p = p_next
        if c == 0 and (r + 1) * sub < tm:
            hs[r + 1] = normed(r + 1)
        if idx + 1 < len(items):
            p_next = project(items[idx + 1])
        y = p
        if c in NORM_CHUNKS:
            ss = jnp.dot((p * p).astype(BF16), bd_ref[...], preferred_element_type=F32)
            pieces = []
            for q0 in range(0, sub, EPILOGUE_ROWS):
                piece = slice(q0, q0 + EPILOGUE_ROWS)
                yq = p[piece] * lax.rsqrt(ss[piece] * (1.0 / HEAD_DIM) + EPS) * gcol_ref[:, cols]
                partner = jnp.where(first_half,
                                    pltpu.roll(yq, CHUNK - HEAD_DIM // 2, 1),
                                    pltpu.roll(yq, HEAD_DIM // 2, 1))
                trows = slice(r * sub + q0, r * sub + q0 + EPILOGUE_ROWS)
                pieces.append(yq * cos_ref[trows, :] + partner * sin_ref[trows, :])
            y = jnp.concatenate(pieces, axis=0)
        if c == KV_CHUNK:
            o_ref[rows, SWA_Q:SWA_Q + CHUNK] = twice(y[:, :LANES]).astype(BF16)
            o_ref[rows, SWA_Q + CHUNK:SWA_Q + 2 * CHUNK] = twice(p[:, LANES:]).astype(BF16)
        else:
            out0 = c * CHUNK if c < KV_CHUNK else (c + 1) * CHUNK
            o_ref[rows, out0:out0 + CHUNK] = y.astype(BF16)


def _inproj(xf, g, w, gcol, cos_t, sin_t, bd, *, layer, seq, tm, sub):
    t = xf.shape[0]
    tiles_per_seq = seq // tm
    return pl.pallas_call(
        functools.partial(_inproj_kernel, sub=sub),
        grid=(t // tm,),
        in_specs=[
            pl.BlockSpec((tm, D_MODEL), lambda i: (i, 0)),
            _layer((1, D_MODEL), layer),
            _layer((D_MODEL, IN_COLS), layer, pipeline_mode=pl.Buffered(1)),
            _layer((1, IN_COLS), layer),
            pl.BlockSpec((tm, CHUNK), lambda i: (i % tiles_per_seq, 0)),
            pl.BlockSpec((tm, CHUNK), lambda i: (i % tiles_per_seq, 0)),
            pl.BlockSpec((CHUNK, CHUNK), lambda i: (0, 0)),
        ],
        out_specs=pl.BlockSpec((tm, PROJ_COLS), lambda i: (i, 0)),
        out_shape=jax.ShapeDtypeStruct((t, PROJ_COLS), BF16),
        compiler_params=pltpu.CompilerParams(
            dimension_semantics=("parallel",), vmem_limit_bytes=INPROJ_VMEM),
        name="inproj",
    )(xf, g, w, gcol, cos_t, sin_t, bd)


def _swa_units(sink_ref, q_ref, k_refs, v_refs, y_ref, *, layer, n, ntiles, tq):
    band = 3 * BLOCK
    blocks = tq // BLOCK
    kp_ref, kc_ref, kn_ref = k_refs
    vp_ref, vc_ref, vn_ref = v_refs
    k_ref_rows = [kp_ref] + [kc_ref] * blocks + [kn_ref]
    v_ref_rows = [vp_ref] + [vc_ref] * blocks + [vn_ref]

    def band_rows(refs, blk, lanes):
        parts = []
        for j in range(blk, blk + 3):
            ref = refs[j]
            parts.append(ref[:, lanes] if j in (0, blocks + 1)
                         else ref[(j - 1) * BLOCK:j * BLOCK, lanes])
        return jnp.concatenate(parts, axis=0)

    row = lax.broadcasted_iota(jnp.int32, (BLOCK, band), 0)
    col = lax.broadcasted_iota(jnp.int32, (BLOCK, band), 1)
    in_window = jnp.abs(col - BLOCK - row) <= WINDOW
    lo = lax.broadcasted_iota(jnp.int32, (BLOCK, LANES), 1) < HEAD_DIM
    hi = jnp.logical_not(lo)
    heads_per_kv = SWA_Q_HEADS // SWA_KV_HEADS
    ones = jnp.ones((band, LANES), BF16)
    zero = jnp.zeros((BLOCK, LANES), BF16)

    def issue(blk, kv):
        stack = []
        for hh in range(heads_per_kv):
            c0 = kv * heads_per_kv * HEAD_DIM + (hh // 2) * LANES
            qh = q_ref[blk * BLOCK:(blk + 1) * BLOCK, c0:c0 + LANES]
            stack.append(jnp.where(lo if hh % 2 == 0 else hi, qh, zero))
        qs = jnp.concatenate(stack, axis=0)
        kk = band_rows(k_ref_rows, blk, slice(kv * LANES, (kv + 1) * LANES))
        return lax.dot_general(qs, kk, (((1,), (1,)), ((), ())), preferred_element_type=F32)

    def finish(blk, kv, s):
        valid = in_window
        if blk == 0:
            valid = valid & ((col >= BLOCK) | (n > 0))
        if blk == blocks - 1:
            valid = valid & ((col < 2 * BLOCK) | (n < ntiles - 1))
        vv = band_rows(v_ref_rows, blk, slice(kv * LANES, (kv + 1) * LANES))
        v1 = jnp.concatenate([vv, ones], axis=1)
        s = jnp.where(valid[None], s.reshape(heads_per_kv, BLOCK, band), NEG)
        s = s.reshape(heads_per_kv * BLOCK, band)
        sk = jnp.concatenate(
            [jnp.full((BLOCK, LANES), sink_ref[layer, kv * heads_per_kv + hh] * LOG2E, F32)
             for hh in range(heads_per_kv)], axis=0)
        m = jnp.maximum(jnp.max(s, axis=-1, keepdims=True), sk)
        p = jnp.concatenate(
            [jnp.exp2(s[:, j * LANES:(j + 1) * LANES] - m) for j in range(band // LANES)],
            axis=1).astype(BF16)
        o2 = jnp.dot(p, v1, preferred_element_type=F32)
        denom = o2[:, LANES:] + jnp.exp2(sk - m)
        o = o2[:, :LANES] / denom
        for pair in range(heads_per_kv // 2):
            out = jnp.where(lo, o[2 * pair * BLOCK:(2 * pair + 1) * BLOCK],
                            o[(2 * pair + 1) * BLOCK:(2 * pair + 2) * BLOCK])
            c0 = kv * heads_per_kv * HEAD_DIM + pair * LANES
            y_ref[blk * BLOCK:(blk + 1) * BLOCK, c0:c0 + LANES] = out.astype(BF16)

    keys = [(blk, kv) for blk in range(blocks) for kv in range(SWA_KV_HEADS)]
    return [(functools.partial(issue, *key), functools.partial(finish, *key)) for key in keys]


def _diff_units(lam, subln_ref, q_ref, k_ref, v1_ref, y_ref, *, lambda_init, tq, rows):
    lo = lax.broadcasted_iota(jnp.int32, (rows, LANES), 1) < HEAD_DIM
    hi = jnp.logical_not(lo)
    zero = jnp.zeros((rows, LANES), BF16)
    held = {}

    def issue(h, r, c):
        qh = q_ref[r * rows:(r + 1) * rows, h * LANES:(h + 1) * LANES]
        qm = jnp.where(lo if c == 0 else hi, qh, zero)
        return lax.dot_general(qm, k_ref[:, h * LANES:(h + 1) * LANES], (((1,), (1,)), ((), ())),
                               preferred_element_type=F32)

    def finish(h, r, c, s):
        m = jnp.max(s, axis=-1, keepdims=True)
        p = jnp.exp2(s - m).astype(BF16)
        o2 = jnp.dot(p, v1_ref[h], preferred_element_type=F32)
        o = o2[:, :DIFF_V_DIM] / o2[:, DIFF_V_DIM:]
        if c == 0:
            held[(h, r)] = o
        else:
            od = held.pop((h, r)) - lam * o
            y = od * _rms(od) * subln_ref[...] * (1.0 - lambda_init)
            c0 = SWA_Q + h * DIFF_V_DIM
            y_ref[r * rows:(r + 1) * rows, c0:c0 + DIFF_V_DIM] = y.astype(BF16)

    keys = [(h, r, c) for h in range(DIFF_HEADS) for r in range(tq // rows) for c in range(2)]
    return [(functools.partial(issue, *key), functools.partial(finish, *key)) for key in keys]


def _attn_kernel(sink_ref, lq1_ref, lk1_ref, lq2_ref, lk2_ref, subln_ref,
                 qa_ref, kp_ref, kc_ref, kn_ref, vp_ref, vc_ref, vn_ref, qb_ref, kb_ref, vb_ref,
                 x_ref, wo_ref, o_ref, v1_ref, y_ref, *, layer, lambda_init, ntiles, tq, rows):
    n = pl.program_id(1)
    lam = (jnp.exp(jnp.sum(lq1_ref[...] * lk1_ref[...], axis=-1, keepdims=True))
           - jnp.exp(jnp.sum(lq2_ref[...] * lk2_ref[...], axis=-1, keepdims=True))
           + lambda_init)

    @pl.when(n == 0)
    def _():
        for h in range(DIFF_HEADS):
            v1_ref[h, :, :DIFF_V_DIM] = vb_ref[:, h * DIFF_V_DIM:(h + 1) * DIFF_V_DIM]
            v1_ref[h, :, DIFF_V_DIM:] = jnp.ones((vb_ref.shape[0], DIFF_V_DIM), BF16)

    units = _swa_units(sink_ref, qa_ref, (kp_ref, kc_ref, kn_ref), (vp_ref, vc_ref, vn_ref), y_ref,
                       layer=layer, n=n, ntiles=ntiles, tq=tq)
    units += _diff_units(lam, subln_ref, qb_ref, kb_ref, v1_ref, y_ref,
                         lambda_init=lambda_init, tq=tq, rows=rows)
    s_next = units[0][0]()
    for idx, (_, finish) in enumerate(units):
        s = s_next
        if idx + 1 < len(units):
            s_next = units[idx + 1][0]()
        finish(s)
    o_ref[...] = x_ref[...] + jnp.dot(y_ref[...], wo_ref[...], preferred_element_type=F32)


def _attn(proj3, x3, sink, lq1, lk1, lq2, lk2, subln, wo, *, layer, lambda_init, tq, rows):
    b, s, _ = proj3.shape
    ntiles = s // tq
    per_tile = tq // BLOCK
    last_block = s // BLOCK - 1
    kcol = SWA_Q // CHUNK
    qcol = (2 * SWA_Q) // DIFF_Q
    prev = lambda col: pl.BlockSpec(
        (None, BLOCK, CHUNK), lambda bi, n: (bi, jnp.maximum(n * per_tile - 1, 0), col))
    own = lambda col: pl.BlockSpec((None, tq, CHUNK), lambda bi, n: (bi, n, col))
    nxt = lambda col: pl.BlockSpec(
        (None, BLOCK, CHUNK), lambda bi, n: (bi, jnp.minimum((n + 1) * per_tile, last_block), col))
    return pl.pallas_call(
        functools.partial(_attn_kernel, layer=layer, lambda_init=lambda_init, ntiles=ntiles, tq=tq,
                          rows=rows),
        grid=(b, ntiles),
        in_specs=[
            pl.BlockSpec(memory_space=pltpu.SMEM),
            _layer((1, HEAD_DIM), layer), _layer((1, HEAD_DIM), layer),
            _layer((1, HEAD_DIM), layer), _layer((1, HEAD_DIM), layer),
            _layer((1, DIFF_V_DIM), layer),
            pl.BlockSpec((None, tq, SWA_Q), lambda bi, n: (bi, n, 0)),
            prev(kcol), own(kcol), nxt(kcol),
            prev(kcol + 1), own(kcol + 1), nxt(kcol + 1),
            pl.BlockSpec((None, tq, DIFF_Q), lambda bi, n: (bi, n, qcol)),
            pl.BlockSpec((None, s, DIFF_Q), lambda bi, n: (bi, 0, qcol + 1)),
            pl.BlockSpec((None, s, DIFF_V), lambda bi, n: (bi, 0, qcol + 2)),
            pl.BlockSpec((None, tq, D_MODEL), lambda bi, n: (bi, n, 0)),
            _layer((SWA_Q + DIFF_V, D_MODEL), layer),
        ],
        out_specs=pl.BlockSpec((None, tq, D_MODEL), lambda bi, n: (bi, n, 0)),
        out_shape=jax.ShapeDtypeStruct((b, s, D_MODEL), F32),
        scratch_shapes=[pltpu.VMEM((DIFF_HEADS, s, 2 * DIFF_V_DIM), BF16),
                        pltpu.VMEM((tq, SWA_Q + DIFF_V), BF16)],
        compiler_params=pltpu.CompilerParams(
            dimension_semantics=("parallel", "arbitrary"), vmem_limit_bytes=ATTN_VMEM),
        name="attn",
    )(sink, lq1, lk1, lq2, lk2, subln, proj3, proj3, proj3, proj3, proj3, proj3, proj3,
      proj3, proj3, proj3, x3, wo)


def _ffn_kernel(xp_ref, xc_ref, xn_ref, g_ref, wu_ref, cw_ref, cb_ref, wd_ref, o_ref,
                hext_ref, *, ts, tiles_per_seq):
    j = pl.program_id(0) % tiles_per_seq
    g = g_ref[...]

    def norm(xx):
        return (xx * _rms(xx) * g).astype(BF16)

    hp = norm(xp_ref[...])
    hn = norm(xn_ref[...])
    hext_ref[0:HALO, :] = jnp.where(j > 0, hp, jnp.zeros_like(hp))
    hext_ref[HALO:HALO + ts, :] = norm(xc_ref[...])
    hext_ref[HALO + ts:, :] = jnp.where(j < tiles_per_seq - 1, hn, jnp.zeros_like(hn))

    chunks = [slice(c0, c0 + FF_CHUNK) for c0 in range(0, D_FF, FF_CHUNK)]

    def up(cols):
        gate = jnp.dot(hext_ref[...], wu_ref[:, cols], preferred_element_type=F32)
        val = jnp.dot(hext_ref[HALO:HALO + ts, :],
                      wu_ref[:, D_FF + cols.start:D_FF + cols.stop], preferred_element_type=F32)
        return gate, val

    ahead = up(chunks[0])
    acts = []
    for c, cols in enumerate(chunks):
        gate, val = ahead
        if c + 1 < len(chunks):
            ahead = up(chunks[c + 1])
        cw = cw_ref[:, cols]
        rows = gate.shape[0]
        g_prev = pltpu.roll(gate, 1, 0)[HALO:HALO + ts]
        g_next = pltpu.roll(gate, rows - 1, 0)[HALO:HALO + ts]
        g_cur = gate[HALO:HALO + ts]
        conv = cb_ref[:, cols] + cw[0:1] * g_prev + cw[1:2] * g_cur + cw[2:3] * g_next
        acts.append((conv * (1.0 / (1.0 + jnp.exp(-conv))) * val).astype(BF16))
    o_ref[...] = xc_ref[...] + jnp.dot(jnp.concatenate(acts, axis=1), wd_ref[...],
                                       preferred_element_type=F32)


def _ffn(xf, g, wu, cw, cb, wd, *, layer, seq, ts):
    t = xf.shape[0]
    tiles_per_seq = seq // ts
    halo_blocks = ts // HALO
    last_halo = t // HALO - 1
    resident = pl.Buffered(1)
    return pl.pallas_call(
        functools.partial(_ffn_kernel, ts=ts, tiles_per_seq=tiles_per_seq),
        grid=(t // ts,),
        in_specs=[
            pl.BlockSpec((HALO, D_MODEL), lambda i: (jnp.maximum(i * halo_blocks - 1, 0), 0)),
            pl.BlockSpec((ts, D_MODEL), lambda i: (i, 0)),
            pl.BlockSpec((HALO, D_MODEL), lambda i: (jnp.minimum((i + 1) * halo_blocks, last_halo), 0)),
            _layer((1, D_MODEL), layer),
            _layer((D_MODEL, 2 * D_FF), layer, pipeline_mode=resident),
            _layer((3, D_FF), layer),
            _layer((1, D_FF), layer),
            _layer((D_FF, D_MODEL), layer, pipeline_mode=resident),
        ],
        out_specs=pl.BlockSpec((ts, D_MODEL), lambda i: (i, 0)),
        out_shape=jax.ShapeDtypeStruct((t, D_MODEL), F32),
        scratch_shapes=[pltpu.VMEM((ts + 2 * HALO, D_MODEL), BF16)],
        compiler_params=pltpu.CompilerParams(
            dimension_semantics=("parallel",), vmem_limit_bytes=FFN_VMEM),
        name="convglu",
    )(xf, xf, xf, g, wu, cw, cb, wd)


def _rope_tables(seq):
    inv = 1.0 / (ROPE_THETA ** (np.arange(0, HEAD_DIM, 2, dtype=np.float64) / HEAD_DIM))
    ang = np.arange(seq, dtype=np.float64)[:, None] * inv[None, :]
    cos, sin = np.cos(ang), np.sin(ang)
    reps = CHUNK // HEAD_DIM
    cos_t = np.tile(np.concatenate([cos, cos], axis=-1), (1, reps))
    sin_t = np.tile(np.concatenate([-sin, sin], axis=-1), (1, reps))
    return jnp.asarray(cos_t, F32), jnp.asarray(sin_t, F32)


def _head_block_diag():
    idx = np.arange(CHUNK) // HEAD_DIM
    return jnp.asarray(idx[:, None] == idx[None, :], BF16)


def kernel(x, g_attn, w_in, qn_a, kn_a, sink, qn_b, kn_b, lq1, lk1, lq2, lk2, subln, w_out, g_ffn,
           w_up, conv_w, conv_b, w_down):
    b, s, d = x.shape
    depth = w_in.shape[0]
    cos_t, sin_t = _rope_tables(s)
    bd = _head_block_diag()
    scale = HEAD_DIM ** -0.5 * LOG2E
    xf = x.reshape(b * s, d)

    row = lambda p: p[:, None, :]
    heads = lambda g, n: jnp.tile(g, (1, n))
    w_proj = w_in.astype(BF16)
    gcol = row(jnp.concatenate([
        heads(qn_a, SWA_Q_HEADS) * scale, heads(kn_a, SWA_KV_HEADS),
        jnp.ones((depth, SWA_KV), F32),
        heads(qn_b, 2 * DIFF_HEADS) * scale, heads(kn_b, 2 * DIFF_HEADS),
        jnp.ones((depth, DIFF_V), F32)], axis=-1))
    wo = w_out.astype(BF16)
    wu = w_up.astype(BF16)
    wd = w_down.astype(BF16)
    g_attn3, g_ffn3, subln3, conv_b3 = row(g_attn), row(g_ffn), row(subln), row(conv_b)
    lq1_3, lk1_3, lq2_3, lk2_3 = row(lq1), row(lk1), row(lq2), row(lk2)

    for l in range(depth):
        lambda_init = 0.8 - 0.6 * math.exp(-0.3 * l)
        proj = _inproj(xf, g_attn3, w_proj, gcol, cos_t, sin_t, bd, layer=l, seq=s,
                       tm=INPROJ_TILE, sub=INPROJ_SUBTILE)
        proj3 = proj.reshape(b, s, PROJ_COLS)
        x3 = _attn(proj3, xf.reshape(b, s, d), sink, lq1_3, lk1_3, lq2_3, lk2_3, subln3, wo,
                   layer=l, lambda_init=lambda_init, tq=ATTN_TILE, rows=DIFF_UNIT_ROWS)
        xf = _ffn(x3.reshape(b * s, d), g_ffn3, wu, conv_w, conv_b3, wd, layer=l, seq=s, ts=FFN_TILE)
    return xf.reshape(b, s, d)
```
